```python
import jax, jax.numpy as jnp
from jax import lax
import numpy as np

D_MODEL = 1024
BATCH = 16
SEQ = 256
DEPTH = 1
DEC_BATCH = 2
DEC_SEQ = 2048
PAST_LEN = 256

GRID_W = 64
POS_THETA = 10000.0
EPS = 1e-6
N_MOD = 6
A_GROUPS = 4
A_GROUP_DIM = 128
A_WIDTH = A_GROUPS * A_GROUP_DIM
CHUNK_MLP = 128
H_B = 4
DK = 128
DV = 128
QK_WIDTH = H_B * DK
V_WIDTH = H_B * DV
CHUNK_REC = 32
IN_SPLITS = (A_WIDTH, A_WIDTH, QK_WIDTH, QK_WIDTH, QK_WIDTH, V_WIDTH, V_WIDTH, D_MODEL, D_MODEL)
IN_WIDTH = A_WIDTH * 2 + QK_WIDTH * 3 + V_WIDTH * 2 + D_MODEL * 2
PEER_HEADS = 8
N_KEYS = 128
N_EXPERTS = N_KEYS * N_KEYS
PEER_TOPK = 16
PEER_QHALF = 128
PEER_QDIM = 2 * PEER_QHALF
PEER_BLOCK = 128

kernel_name = "hybrid_gmlp_hgrn2_peer_diffusion_step"


def rmsnorm(x, g):
    xf = x.astype(jnp.float32)
    y = xf * lax.rsqrt(jnp.mean(xf * xf, axis=-1, keepdims=True) + EPS)
    return (y * g.astype(jnp.float32)).astype(x.dtype)


def grid_pos_embed(n_tokens):
    rows = n_tokens // GRID_W
    r = jnp.repeat(jnp.arange(rows, dtype=jnp.float32), GRID_W)
    col = jnp.tile(jnp.arange(GRID_W, dtype=jnp.float32), rows)
    quarter = D_MODEL // 4
    omega = 1.0 / (POS_THETA ** (jnp.arange(quarter, dtype=jnp.float32) / quarter))
    ar = r[:, None] * omega
    ac = col[:, None] * omega
    return jnp.concatenate([jnp.sin(ar), jnp.cos(ar), jnp.sin(ac), jnp.cos(ac)], axis=-1)


def adaln(cond, w_ada, b_ada):
    m = jax.nn.silu(cond) @ w_ada + b_ada
    return m.reshape(-1, N_MOD, D_MODEL)


def chunk_gmlp(zu, zv, norm_g, w_s, b_s):
    B, T, _ = zu.shape
    n = T // CHUNK_MLP
    u = jax.nn.gelu(zu)
    v = rmsnorm(jax.nn.gelu(zv), norm_g)
    vc = v.reshape(B, n, CHUNK_MLP, A_GROUPS, A_GROUP_DIM)
    vs = jnp.einsum('gts,bnsgc->bntgc', w_s, vc) + b_s.T[None, None, :, :, None]
    return u * vs.reshape(B, T, A_WIDTH)


def hgrn2_chunk_scan(q, k, v, logf, s0):
    B, H, T, _ = q.shape
    n = T // CHUNK_REC

    def to_chunks(a):
        return a.reshape(B, H, n, CHUNK_REC, a.shape[-1]).transpose(2, 0, 1, 3, 4)

    mask = jnp.tril(jnp.ones((CHUNK_REC, CHUNK_REC), dtype=bool))[:, :, None]

    def step(S, inp):
        qc, kc, vc, lfc = inp
        b = jnp.cumsum(lfc, axis=-2)
        diff = b[..., :, None, :] - b[..., None, :, :]
        decay = jnp.exp(jnp.where(mask, diff, -jnp.inf))
        scores = jnp.einsum('bhtk,bhsk,bhtsk->bhts', qc, kc, decay)
        o = (jnp.einsum('bhts,bhsv->bhtv', scores, vc)
             + jnp.einsum('bhtk,bhkv->bhtv', qc * jnp.exp(b), S))
        b_last = b[..., -1:, :]
        S_new = (jnp.exp(b_last[..., 0, :])[..., None] * S
                 + jnp.einsum('bhsk,bhsv->bhkv', kc * jnp.exp(b_last - b), vc))
        return S_new, o

    s_fin, o = lax.scan(step, s0, (to_chunks(q), to_chunks(k), to_chunks(v), to_chunks(logf)))
    o = o.transpose(1, 2, 0, 3, 4).reshape(B, H, T, v.shape[-1])
    return o, s_fin


def hgrn2_bidir(zq, zf_fw, zf_bw, zi, zg, lb, norm_g, s0):
    B, T, _ = zq.shape

    def heads(a):
        return a.reshape(B, T, H_B, -1).transpose(0, 2, 1, 3).astype(jnp.float32)

    def gates(zf, lb_d):
        f = lb_d + (1.0 - lb_d) * jax.nn.sigmoid(zf.astype(jnp.float32))
        return heads(1.0 - f), heads(jnp.log(f))

    q = heads(zq)
    v = heads(zi)
    k_fw, lf_fw = gates(zf_fw, lb[0])
    k_bw, lf_bw = gates(zf_bw, lb[1])
    o_fw, s_fw = hgrn2_chunk_scan(q, k_fw, v, lf_fw, s0[:, 0])
    flip = lambda a: jnp.flip(a, axis=2)
    o_bw, s_bw = hgrn2_chunk_scan(flip(q), flip(k_bw), flip(v), flip(lf_bw), s0[:, 1])
    o = (o_fw + flip(o_bw)).transpose(0, 2, 1, 3)
    o = rmsnorm(o, norm_g) * jax.nn.silu(zg.reshape(B, T, H_B, DV).astype(jnp.float32))
    return o.reshape(B, T, V_WIDTH).astype(zq.dtype), jnp.stack([s_fw, s_bw], axis=1)


def peer(x, w_q, sub_keys, u_tab, v_tab):
    shp = x.shape
    xt = x.reshape(-1, D_MODEL)
    n_tok = xt.shape[0]
    q = (xt @ w_q).reshape(n_tok, PEER_HEADS, 2, PEER_QHALF)
    s = jnp.einsum('thpc,hpkc->thpk', q, sub_keys).astype(jnp.float32)
    v1, i1 = lax.top_k(s[:, :, 0], PEER_TOPK)
    v2, i2 = lax.top_k(s[:, :, 1], PEER_TOPK)
    cand = (v1[..., :, None] + v2[..., None, :]).reshape(n_tok, PEER_HEADS, PEER_TOPK * PEER_TOPK)
    cand_idx = (i1[..., :, None] * N_KEYS + i2[..., None, :]).reshape(n_tok, PEER_HEADS, PEER_TOPK * PEER_TOPK)
    top_v, top_j = lax.top_k(cand, PEER_TOPK)
    idx = jnp.take_along_axis(cand_idx, top_j, axis=-1).reshape(n_tok, PEER_HEADS * PEER_TOPK)
    g = jax.nn.softmax(top_v, axis=-1).reshape(n_tok, PEER_HEADS * PEER_TOPK).astype(x.dtype)
    nb = n_tok // PEER_BLOCK

    def block(args):
        xb, ib, gb = args
        ue = jnp.take(u_tab, ib, axis=0)
        ve = jnp.take(v_tab, ib, axis=0)
        a = jax.nn.gelu(jnp.einsum('bd,bed->be', xb, ue)) * gb
        return jnp.einsum('be,bed->bd', a, ve)

    out = lax.map(block, (xt.reshape(nb, PEER_BLOCK, D_MODEL),
                          idx.reshape(nb, PEER_BLOCK, -1),
                          g.reshape(nb, PEER_BLOCK, -1)))
    return out.reshape(shp)


def trunk_layer(x, mod, s0, lb, norm1_g, w_in, sgu_norm_g, w_spatial, b_spatial, hgrn_norm_g,
                w_proj_a, w_proj_b, w_out, norm2_g, peer_w_q, peer_sub_keys, peer_u, peer_v):
    shift1, scale1, gate1, shift2, scale2, gate2 = [mod[:, i][:, None, :] for i in range(N_MOD)]
    h = rmsnorm(x, norm1_g) * (1.0 + scale1) + shift1
    z = h @ w_in
    offsets = []
    acc = 0
    for s in IN_SPLITS[:-1]:
        acc += s
        offsets.append(acc)
    zu, zv, zq, zf_fw, zf_bw, zi, zg, za, zb = jnp.split(z, offsets, axis=-1)
    y_a = chunk_gmlp(zu, zv, sgu_norm_g, w_spatial, b_spatial)
    y_b, s_fin = hgrn2_bidir(zq, zf_fw, zf_bw, zi, zg, lb, hgrn_norm_g, s0)
    mix = (jax.nn.sigmoid(za) * (y_a @ w_proj_a) + jax.nn.sigmoid(zb) * (y_b @ w_proj_b)) @ w_out
    x = x + gate1 * mix
    h = rmsnorm(x, norm2_g) * (1.0 + scale2) + shift2
    x = x + gate2 * peer(h, peer_w_q, peer_sub_keys, peer_u, peer_v)
    return x, s_fin


def setup_inputs(seed: int = 0) -> dict:
    key = jax.random.key(seed)
    ks = jax.random.split(key, 24)
    nrm = lambda k, shape, scale: jax.random.normal(k, shape, jnp.float32) * scale
    Dinv = D_MODEL ** -0.5
    return {
        "x_prompt": nrm(ks[0], (BATCH, SEQ, D_MODEL), 1.0),
        "x_sample": nrm(ks[1], (DEC_BATCH, DEC_SEQ, D_MODEL), 1.0),
        "state_hgrn": nrm(ks[2], (DEC_BATCH, DEPTH, 2, H_B, DK, DV), 0.5),
        "c": nrm(ks[3], (DEC_BATCH, D_MODEL), 1.0),
        "c_ctx": nrm(ks[4], (D_MODEL,), 1.0),
        "w_ada": nrm(ks[5], (DEPTH, D_MODEL, N_MOD * D_MODEL), Dinv),
        "b_ada": nrm(ks[6], (DEPTH, N_MOD * D_MODEL), 0.02),
        "norm1_g": 1.0 + nrm(ks[7], (DEPTH, D_MODEL), 0.02),
        "w_in": nrm(ks[8], (DEPTH, D_MODEL, IN_WIDTH), Dinv),
        "sgu_norm_g": 1.0 + nrm(ks[9], (DEPTH, A_WIDTH), 0.02),
        "w_spatial": nrm(ks[10], (DEPTH, A_GROUPS, CHUNK_MLP, CHUNK_MLP), CHUNK_MLP ** -0.5),
        "b_spatial": 1.0 + nrm(ks[11], (DEPTH, A_GROUPS, CHUNK_MLP), 0.02),
        "hgrn_lb": nrm(ks[12], (DEPTH + 1, 2, QK_WIDTH), 0.5),
        "hgrn_norm_g": 1.0 + nrm(ks[13], (DEPTH, H_B, DV), 0.02),
        "w_proj_a": nrm(ks[14], (DEPTH, A_WIDTH, D_MODEL), A_WIDTH ** -0.5),
        "w_proj_b": nrm(ks[15], (DEPTH, V_WIDTH, D_MODEL), V_WIDTH ** -0.5),
        "w_out": nrm(ks[16], (DEPTH, D_MODEL, D_MODEL), Dinv),
        "norm2_g": 1.0 + nrm(ks[17], (DEPTH, D_MODEL), 0.02),
        "peer_w_q": nrm(ks[18], (DEPTH, D_MODEL, PEER_HEADS * PEER_QDIM), Dinv),
        "peer_sub_keys": nrm(ks[19], (DEPTH, PEER_HEADS, 2, N_KEYS, PEER_QHALF), PEER_QHALF ** -0.5),
        "peer_u": nrm(ks[20], (DEPTH, N_EXPERTS, D_MODEL), Dinv),
        "peer_v": nrm(ks[21], (DEPTH, N_EXPERTS, D_MODEL), (PEER_HEADS * PEER_TOPK) ** -0.5),
        "final_norm_g": 1.0 + nrm(ks[22], (D_MODEL,), 0.02),
    }


def reference(x_prompt, x_sample, state_hgrn, c, c_ctx, w_ada, b_ada, norm1_g, w_in, sgu_norm_g,
              w_spatial, b_spatial, hgrn_lb, hgrn_norm_g, w_proj_a, w_proj_b, w_out, norm2_g,
              peer_w_q, peer_sub_keys, peer_u, peer_v, final_norm_g):
    lb = jnp.cumsum(jax.nn.softmax(hgrn_lb.astype(jnp.float32), axis=0), axis=0)[:DEPTH]
    xp = x_prompt
    xs = x_sample + grid_pos_embed(x_sample.shape[1]).astype(x_sample.dtype)[None]
    s_ctx0 = jnp.zeros((x_prompt.shape[0], 2, H_B, DK, DV), jnp.float32)
    ctx_states = []
    for l in range(DEPTH):
        params = (norm1_g[l], w_in[l], sgu_norm_g[l], w_spatial[l], b_spatial[l], hgrn_norm_g[l],
                  w_proj_a[l], w_proj_b[l], w_out[l], norm2_g[l], peer_w_q[l], peer_sub_keys[l],
                  peer_u[l], peer_v[l])
        mod_ctx = adaln(c_ctx[None].astype(xp.dtype), w_ada[l], b_ada[l])
        mod_lat = adaln(c, w_ada[l], b_ada[l])
        xp, s_ctx = trunk_layer(xp, mod_ctx, s_ctx0, lb[l], *params)
        xs, _ = trunk_layer(xs, mod_lat, state_hgrn[:, l].astype(jnp.float32), lb[l], *params)
        ctx_states.append(s_ctx)
    y_prompt = rmsnorm(xp, final_norm_g)
    y_sample = rmsnorm(xs, final_norm_g)
    new_state_hgrn = jnp.stack(ctx_states, axis=1)
    return (y_prompt, y_sample, new_state_hgrn)
```

```python
import functools

import jax
import jax.numpy as jnp
from jax import lax
from jax.experimental import pallas as pl
from jax.experimental.pallas import tpu as pltpu

F32 = jnp.float32
BF16 = jnp.bfloat16
I32 = jnp.int32

D_MODEL = 1024
N_MOD = 6
A_WIDTH = 512
A_GROUPS = 4
CHUNK_MLP = 128
H_B = 4
DK = 128
DV = 128
QK_WIDTH = H_B * DK
IN_WIDTH = 5632
COL_BLOCK = 512
PEER_HEADS = 8
N_KEYS = 128
PEER_TOPK = 16
PEER_Q = 2 * PEER_HEADS * 128
N_EXPERTS = N_KEYS * N_KEYS
GRID_W = 64
POS_THETA = 10000.0
EPS = 1e-6

SUBLANES = 8
LANES = 128
VMEM_LIMIT_BYTES = 56 * 1024 * 1024

TB_IN = 512
N_SPLIT_IN = 2
CH = 128
DIAG = 8
TB_MERGE = 256
TB_TOPK = 256
TB_UP = 1024
TB_DOWN = 512
E_BLOCK = 1024
A_STRIDE = 136


def _sds(shape, dtype):
    return jax.ShapeDtypeStruct(shape, dtype)


def _params(*sem):
    return pltpu.CompilerParams(dimension_semantics=sem, vmem_limit_bytes=VMEM_LIMIT_BYTES)


def _dot(a, b):
    return jnp.dot(a, b, preferred_element_type=F32)


def _dot_nt(a, b):
    return lax.dot_general(a, b, (((1,), (1,)), ((), ())), preferred_element_type=F32)


def _dot_tn(a, b):
    return lax.dot_general(a, b, (((0,), (0,)), ((), ())), preferred_element_type=F32)


def _rms(x):
    return x * lax.rsqrt(jnp.mean(x * x, axis=-1, keepdims=True) + EPS)


def _adaln_kernel(c_ref, w_ref, b_ref, o_ref):
    a = jax.nn.silu(c_ref[...])
    o_ref[...] = jnp.dot(a, w_ref[...], preferred_element_type=F32,
                         precision=lax.Precision.HIGHEST) + b_ref[...]


def _adaln(cond8, w_ada, b_ada):
    n = w_ada.shape[1]
    tn = 1536
    return pl.pallas_call(
        _adaln_kernel,
        grid=(n // tn,),
        in_specs=[pl.BlockSpec((SUBLANES, D_MODEL), lambda j: (0, 0)),
                  pl.BlockSpec((D_MODEL, tn), lambda j: (0, j)),
                  pl.BlockSpec((1, tn), lambda j: (0, j))],
        out_specs=pl.BlockSpec((SUBLANES, tn), lambda j: (0, j)),
        out_shape=_sds((SUBLANES, n), F32),
        compiler_params=_params("arbitrary"),
        name="adaln",
    )(cond8, w_ada, b_ada)


def _inproj_kernel(*refs, add_pos):
    if add_pos:
        x_ref, pos_ref, mod_ref, g_ref, w_ref, z_ref, h_scr = refs
    else:
        x_ref, mod_ref, g_ref, w_ref, z_ref, h_scr = refs

    @pl.when(pl.program_id(1) == 0)
    def _():
        x = x_ref[...]
        if add_pos:
            x = x + pos_ref[...]
        h = _rms(x) * g_ref[...] * (1.0 + mod_ref[1:2, :]) + mod_ref[0:1, :]
        h_scr[...] = h.astype(BF16)

    z_ref[...] = _dot(h_scr[...], w_ref[...])


def _inproj(x2d, pos, mod, mod_row, norm_g, w_in_bf, seq_len):
    t = x2d.shape[0]
    bps = seq_len // TB_IN
    tn = IN_WIDTH // N_SPLIT_IN
    add_pos = pos is not None
    in_specs = [pl.BlockSpec((TB_IN, D_MODEL), lambda i, j: (i, 0))]
    args = [x2d]
    if add_pos:
        in_specs.append(pl.BlockSpec((TB_IN, D_MODEL), lambda i, j: (i % bps, 0)))
        args.append(pos)
    in_specs += [pl.BlockSpec((None, N_MOD, D_MODEL), lambda i, j: (mod_row(i * TB_IN), 0, 0)),
                 pl.BlockSpec((1, D_MODEL), lambda i, j: (0, 0)),
                 pl.BlockSpec((D_MODEL, tn), lambda i, j: (0, j))]
    args += [mod, norm_g, w_in_bf]
    return pl.pallas_call(
        functools.partial(_inproj_kernel, add_pos=add_pos),
        grid=(t // TB_IN, N_SPLIT_IN),
        in_specs=in_specs,
        out_specs=pl.BlockSpec((TB_IN, tn), lambda i, j: (i, j)),
        out_shape=_sds((t, IN_WIDTH), F32),
        scratch_shapes=[pltpu.VMEM((TB_IN, D_MODEL), BF16)],
        compiler_params=_params("arbitrary", "arbitrary"),
        name="inproj",
    )(*args)


def _split3(x):
    hi = x.astype(BF16)
    r1 = x - hi.astype(F32)
    mid = r1.astype(BF16)
    lo = (r1 - mid.astype(F32)).astype(BF16)
    return hi, mid, lo


def _hgrn_kernel(zq_ref, zf_ref, zv_ref, lb_ref, s0_ref, o_ref, sfin_ref,
                 st_scr, b_scr, k_scr, *, reverse, n_chunk):
    c = pl.program_id(1)

    @pl.when(c == 0)
    def _():
        st_scr[...] = s0_ref[...]

    lb = lb_ref[...]
    f = lb + (1.0 - lb) * jax.nn.sigmoid(zf_ref[...])
    k_scr[...] = 1.0 - f
    lf = jnp.log(f)
    row = lax.broadcasted_iota(I32, (CH, CH), 0)
    col = lax.broadcasted_iota(I32, (CH, CH), 1)
    tri = jnp.where((row <= col) if reverse else (row >= col), 1.0, 0.0).astype(BF16)
    hi, mid, lo = _split3(lf)
    b_scr[...] = _dot(tri, hi) + _dot(tri, mid) + _dot(tri, lo)

    lane = lax.broadcasted_iota(I32, (DIAG, LANES), 1)
    sub = lax.broadcasted_iota(I32, (DIAG, 1), 0)
    zero_tile = jnp.zeros((DIAG, DK), F32)
    n_tiles = CH // DIAG
    end_row = 0 if reverse else CH - 1

    for h in range(H_B):
        sl = slice(h * DK, (h + 1) * DK)
        tiles = []
        for j in range(n_tiles):
            r0 = j * DIAG
            qt = zq_ref[r0:r0 + DIAG, sl]
            bt = b_scr[r0:r0 + DIAG, sl]
            acc = zero_tile
            for s in range(DIAG):
                r = r0 + s
                d = jnp.minimum(bt - b_scr[r:r + 1, sl], 0.0)
                e = qt * jnp.exp(d) * k_scr[r:r + 1, sl]
                colv = jnp.sum(e, axis=-1, keepdims=True)
                colv = jnp.where((sub <= s) if reverse else (sub >= s), colv, 0.0)
                acc = jnp.where(lane == r, colv, acc)
            tiles.append(acc)
        s_tot = jnp.concatenate(tiles, axis=0)

        m = DIAG
        while m < CH:
            q_tiles, k_tiles = [], []
            for j in range(n_tiles):
                r0 = j * DIAG
                blk = r0 // m
                gives_q = (blk % 2 == 0) if reverse else (blk % 2 == 1)
                if gives_q:
                    ref_row = blk * m + m if reverse else blk * m - 1
                    qt = zq_ref[r0:r0 + DIAG, sl] * jnp.exp(
                        b_scr[r0:r0 + DIAG, sl] - b_scr[ref_row:ref_row + 1, sl])
                    q_tiles.append(qt)
                    k_tiles.append(zero_tile)
                else:
                    ref_row = blk * m if reverse else blk * m + m - 1
                    kt = k_scr[r0:r0 + DIAG, sl] * jnp.exp(
                        b_scr[ref_row:ref_row + 1, sl] - b_scr[r0:r0 + DIAG, sl])
                    k_tiles.append(kt)
                    q_tiles.append(zero_tile)
            qm = jnp.concatenate(q_tiles, axis=0).astype(BF16)
            km = jnp.concatenate(k_tiles, axis=0).astype(BF16)
            sc = _dot_nt(qm, km)
            if 2 * m < CH:
                shift = (2 * m).bit_length() - 1
                sc = jnp.where((row >> shift) == (col >> shift), sc, 0.0)
            s_tot = s_tot + sc
            m *= 2

        q = zq_ref[:, sl]
        v = zv_ref[:, sl].astype(BF16)
        bh = b_scr[:, sl]
        st = st_scr[h]
        o_h = _dot(s_tot.astype(BF16), v) + _dot_nt((q * jnp.exp(bh)).astype(BF16), st.astype(BF16))
        o_ref[:, sl] = o_h
        b_end = b_scr[end_row:end_row + 1, sl]
        ke = (k_scr[:, sl] * jnp.exp(b_end - bh)).astype(BF16)
        st_scr[h] = st * jnp.exp(b_end) + _dot_tn(v, ke)

    @pl.when(c == n_chunk - 1)
    def _():
        sfin_ref[...] = st_scr[...]


def _hgrn(z, lb_row, s0t, n_seq, seq_len, reverse):
    n_chunk = seq_len // CH
    t = n_seq * seq_len

    def rows(s, c):
        return s * n_chunk + ((n_chunk - 1 - c) if reverse else c)

    zf_col = 4 if reverse else 3
    state_spec = pl.BlockSpec((None, H_B, DV, DK), lambda s, c: (s, 0, 0, 0))
    return pl.pallas_call(
        functools.partial(_hgrn_kernel, reverse=reverse, n_chunk=n_chunk),
        grid=(n_seq, n_chunk),
        in_specs=[pl.BlockSpec((CH, COL_BLOCK), lambda s, c: (rows(s, c), 2)),
                  pl.BlockSpec((CH, COL_BLOCK), lambda s, c: (rows(s, c), zf_col)),
                  pl.BlockSpec((CH, COL_BLOCK), lambda s, c: (rows(s, c), 5)),
                  pl.BlockSpec((1, QK_WIDTH), lambda s, c: (0, 0)),
                  state_spec],
        out_specs=[pl.BlockSpec((CH, COL_BLOCK), lambda s, c: (rows(s, c), 0)), state_spec],
        out_shape=[_sds((t, H_B * DV), F32), _sds((n_seq, H_B, DV, DK), F32)],
        scratch_shapes=[pltpu.VMEM((H_B, DV, DK), F32),
                        pltpu.VMEM((CH, QK_WIDTH), F32),
                        pltpu.VMEM((CH, QK_WIDTH), F32)],
        compiler_params=_params("arbitrary", "arbitrary"),
        name="hgrn_bwd" if reverse else "hgrn_fwd",
    )(z, z, z, lb_row, s0t)


def _merge_kernel(*refs, add_pos):
    refs = list(refs)
    x_ref = refs.pop(0)
    pos_ref = refs.pop(0) if add_pos else None
    (zu_ref, zv_ref, zg_ref, za0_ref, za1_ref, zb0_ref, zb1_ref, ofw_ref, obw_ref, mod_ref,
     sgug_ref, ws_ref, bst_ref, hng_ref, wpa_ref, wpb_ref, wout_ref, n2g_ref,
     x1_ref, hn2_ref) = refs

    u = jax.nn.gelu(zu_ref[...])
    v = (_rms(jax.nn.gelu(zv_ref[...])) * sgug_ref[...]).astype(BF16)
    row_parts = []
    for ci in range(TB_MERGE // CHUNK_MLP):
        col_parts = []
        for g in range(A_GROUPS):
            vg = v[ci * CHUNK_MLP:(ci + 1) * CHUNK_MLP, g * 128:(g + 1) * 128]
            col_parts.append(_dot(ws_ref[g], vg) + bst_ref[:, g:g + 1])
        row_parts.append(jnp.concatenate(col_parts, axis=1))
    y_a = u * jnp.concatenate(row_parts, axis=0)

    o = ofw_ref[...] + obw_ref[...]
    on = jnp.concatenate([_rms(o[:, h * DV:(h + 1) * DV]) for h in range(H_B)], axis=1)
    y_b = on * hng_ref[...] * jax.nn.silu(zg_ref[...])

    pa = _dot(y_a.astype(BF16), wpa_ref[...])
    pb = _dot(y_b.astype(BF16), wpb_ref[...])
    za = jnp.concatenate([za0_ref[...], za1_ref[...]], axis=1)
    zb = jnp.concatenate([zb0_ref[...], zb1_ref[...]], axis=1)
    mix_in = jax.nn.sigmoid(za) * pa + jax.nn.sigmoid(zb) * pb
    mix = _dot(mix_in.astype(BF16), wout_ref[...])

    x = x_ref[...]
    if add_pos:
        x = x + pos_ref[...]
    x1 = x + mod_ref[2:3, :] * mix
    x1_ref[...] = x1
    hn2 = _rms(x1) * n2g_ref[...] * (1.0 + mod_ref[4:5, :]) + mod_ref[3:4, :]
    hn2_ref[...] = hn2.astype(BF16)


def _merge(x2d, pos, z, o_fw, o_bw, mod, mod_row, seq_len, sgu_g, ws_bf, bs_t, hgrn_g,
           wpa_bf, wpb_bf, wout_bf, norm2_g):
    t = x2d.shape[0]
    tb = TB_MERGE
    bps = seq_len // tb
    add_pos = pos is not None

    def zcol(k):
        return pl.BlockSpec((tb, COL_BLOCK), lambda i: (i, k))

    def full(shape):
        return pl.BlockSpec(shape, lambda i: (0,) * len(shape))

    in_specs = [pl.BlockSpec((tb, D_MODEL), lambda i: (i, 0))]
    args = [x2d]
    if add_pos:
        in_specs.append(pl.BlockSpec((tb, D_MODEL), lambda i: (i % bps, 0)))
        args.append(pos)
    in_specs += [zcol(0), zcol(1), zcol(6), zcol(7), zcol(8), zcol(9), zcol(10),
                 pl.BlockSpec((tb, A_WIDTH), lambda i: (i, 0)),
                 pl.BlockSpec((tb, A_WIDTH), lambda i: (i, 0)),
                 pl.BlockSpec((None, N_MOD, D_MODEL), lambda i: (mod_row(i * tb), 0, 0)),
                 full((1, A_WIDTH)), full((A_GROUPS, CHUNK_MLP, CHUNK_MLP)),
                 full((CHUNK_MLP, A_GROUPS)), full((1, H_B * DV)),
                 full((A_WIDTH, D_MODEL)), full((H_B * DV, D_MODEL)),
                 full((D_MODEL, D_MODEL)), full((1, D_MODEL))]
    args += [z] * 7 + [o_fw, o_bw, mod, sgu_g, ws_bf, bs_t, hgrn_g, wpa_bf, wpb_bf, wout_bf, norm2_g]
    return pl.pallas_call(
        functools.partial(_merge_kernel, add_pos=add_pos),
        grid=(t // tb,),
        in_specs=in_specs,
        out_specs=[pl.BlockSpec((tb, D_MODEL), lambda i: (i, 0)),
                   pl.BlockSpec((tb, D_MODEL), lambda i: (i, 0))],
        out_shape=[_sds((t, D_MODEL), F32), _sds((t, D_MODEL), BF16)],
        compiler_params=_params("arbitrary"),
        name="merge",
    )(*args)


def _topk_rows(s, ids, k, big, val_ref, pick):
    for r in range(k):
        m = jnp.max(s, axis=0, keepdims=True)
        first = jnp.min(jnp.where(s == m, ids, big), axis=0, keepdims=True)
        sel = ids == first
        val_ref[r:r + 1, :] = m
        pick(r, sel, first)
        s = jnp.where(sel, -jnp.inf, s)


def _topk_kernel(hn_ref, wq_ref, keys_ref, idx_ref, g_ref, v_scr, i_scr, tv_scr):
    tb = hn_ref.shape[0]
    q = _dot(hn_ref[...], wq_ref[...]).astype(BF16)
    key_ids = lax.broadcasted_iota(I32, (N_KEYS, tb), 0)
    sub8 = lax.broadcasted_iota(I32, (SUBLANES, tb), 0)

    for h in range(PEER_HEADS):
        for p in range(2):
            hp = 2 * h + p
            s = _dot_nt(keys_ref[hp], q[:, hp * 128:(hp + 1) * 128])

            def pick1(r, sel, first, p=p):
                i_scr[p, r:r + 1, :] = first

            _topk_rows(s, key_ids, PEER_TOPK, N_KEYS, v_scr.at[p], pick1)

        vals, eidx, flat = [], [], []
        for r in range(SUBLANES):
            val = v_scr[0, r:r + 1, :] + v_scr[1, 0:8, :]
            vals.append(jnp.where((r + 1) * (sub8 + 1) <= PEER_TOPK, val, -jnp.inf))
            eidx.append(i_scr[0, r:r + 1, :] * N_KEYS + i_scr[1, 0:8, :])
            flat.append(r * PEER_TOPK + sub8)
        vals.append(v_scr[0, 0:1, :] + v_scr[1, 8:16, :])
        eidx.append(i_scr[0, 0:1, :] * N_KEYS + i_scr[1, 8:16, :])
        flat.append(8 + sub8)
        vals.append(v_scr[0, 8:16, :] + v_scr[1, 0:1, :])
        eidx.append(i_scr[0, 8:16, :] * N_KEYS + i_scr[1, 0:1, :])
        flat.append((8 + sub8) * PEER_TOPK)
        cand = jnp.concatenate(vals, axis=0)
        cand_e = jnp.concatenate(eidx, axis=0)
        cand_f = jnp.concatenate(flat, axis=0)

        def pick2(j, sel, first, h=h, cand_e=cand_e):
            e = jnp.max(jnp.where(sel, cand_e, -1), axis=0, keepdims=True)
            idx_ref[h * PEER_TOPK + j:h * PEER_TOPK + j + 1, :] = e

        _topk_rows(cand, cand_f, PEER_TOPK, PEER_TOPK * PEER_TOPK, tv_scr, pick2)
        tv = tv_scr[...]
        ex = jnp.exp(tv - tv[0:1, :])
        g_ref[h * PEER_TOPK:(h + 1) * PEER_TOPK, :] = ex / jnp.sum(ex, axis=0, keepdims=True)


def _topk(hn2, wq_bf, keys_bf):
    t = hn2.shape[0]
    tb = TB_TOPK
    hk = PEER_HEADS * PEER_TOPK
    return pl.pallas_call(
        _topk_kernel,
        grid=(t // tb,),
        in_specs=[pl.BlockSpec((tb, D_MODEL), lambda i: (i, 0)),
                  pl.BlockSpec((D_MODEL, PEER_Q), lambda i: (0, 0)),
                  pl.BlockSpec((2 * PEER_HEADS, N_KEYS, 128), lambda i: (0, 0, 0))],
        out_specs=[pl.BlockSpec((hk, tb), lambda i: (0, i)),
                   pl.BlockSpec((hk, tb), lambda i: (0, i))],
        out_shape=[_sds((hk, t), I32), _sds((hk, t), F32)],
        scratch_shapes=[pltpu.VMEM((2, PEER_TOPK, tb), F32),
                        pltpu.VMEM((2, PEER_TOPK, tb), I32),
                        pltpu.VMEM((PEER_TOPK, tb), F32)],
        compiler_params=_params("arbitrary"),
        name="peer_topk",
    )(hn2, wq_bf, keys_bf)


def _peer_up_kernel(hn_ref, u_ref, idx_ref, p_ref):
    j = pl.program_id(1)

    @pl.when(j == 0)
    def _():
        p_ref[...] = jnp.zeros_like(p_ref)

    s = _dot_nt(hn_ref[...], u_ref[...])
    rc = 128
    for c in range(TB_UP // rc):
        rows = slice(c * rc, (c + 1) * rc)
        idx = idx_ref[rows, :]
        i1 = idx >> 7
        i2 = idx & (N_KEYS - 1)
        acc = p_ref[rows, :]
        for k in range(E_BLOCK // N_KEYS):
            picked = jnp.take_along_axis(s[rows, k * N_KEYS:(k + 1) * N_KEYS], i2, axis=1)
            acc = jnp.where(i1 == j * (E_BLOCK // N_KEYS) + k, picked, acc)
        p_ref[rows, :] = acc


def _peer_up(hn2, u_bf, idx):
    t = hn2.shape[0]
    hk = PEER_HEADS * PEER_TOPK
    return pl.pallas_call(
        _peer_up_kernel,
        grid=(t // TB_UP, N_EXPERTS // E_BLOCK),
        in_specs=[pl.BlockSpec((TB_UP, D_MODEL), lambda i, j: (i, 0)),
                  pl.BlockSpec((E_BLOCK, D_MODEL), lambda i, j: (j, 0)),
                  pl.BlockSpec((TB_UP, hk), lambda i, j: (i, 0))],
        out_specs=pl.BlockSpec((TB_UP, hk), lambda i, j: (i, 0)),
        out_shape=_sds((t, hk), F32),
        compiler_params=_params("arbitrary", "arbitrary"),
        name="peer_up",
    )(hn2, u_bf, idx)


def _peer_down_kernel(p_ref, g_ref, idx_ref, v_ref, o_ref, a_scr, abuf):
    j = pl.program_id(1)

    @pl.when(j == 0)
    def _():
        a_scr[...] = jax.nn.gelu(p_ref[...]) * g_ref[...]
        key_ids = lax.broadcasted_iota(I32, (N_KEYS, PEER_HEADS * PEER_TOPK), 0)

        def body(t, carry):
            idx = idx_ref[pl.ds(t, 1), :]
            pm = jnp.where((idx >> 7) == key_ids, a_scr[pl.ds(t, 1), :], 0.0).astype(BF16)
            qm = jnp.where((idx & (N_KEYS - 1)) == key_ids, 1.0, 0.0).astype(BF16)
            abuf[pl.ds(pl.multiple_of(t * A_STRIDE, SUBLANES), N_KEYS), :] = _dot_nt(pm, qm)
            return carry

        lax.fori_loop(0, TB_DOWN, body, 0)
        o_ref[...] = jnp.zeros_like(o_ref)

    per = E_BLOCK // N_KEYS
    cols = [abuf[pl.ds(j * per + k, TB_DOWN, stride=A_STRIDE), :] for k in range(per)]
    a_blk = jnp.concatenate(cols, axis=1).astype(BF16)
    o_ref[...] += _dot(a_blk, v_ref[...])


def _peer_down(p, g, idx, v_bf):
    t = p.shape[0]
    hk = PEER_HEADS * PEER_TOPK
    tok = pl.BlockSpec((TB_DOWN, hk), lambda i, j: (i, 0))
    return pl.pallas_call(
        _peer_down_kernel,
        grid=(t // TB_DOWN, N_EXPERTS // E_BLOCK),
        in_specs=[tok, tok, tok, pl.BlockSpec((E_BLOCK, D_MODEL), lambda i, j: (j, 0))],
        out_specs=pl.BlockSpec((TB_DOWN, D_MODEL), lambda i, j: (i, 0)),
        out_shape=_sds((t, D_MODEL), F32),
        scratch_shapes=[pltpu.VMEM((TB_DOWN, hk), F32),
                        pltpu.VMEM((TB_DOWN * A_STRIDE, N_KEYS), F32)],
        compiler_params=_params("arbitrary", "arbitrary"),
        name="peer_down",
    )(p, g, idx, v_bf)


def _final_kernel(x1_ref, po_ref, mod_ref, g_ref, y_ref):
    x2 = x1_ref[...] + mod_ref[5:6, :] * po_ref[...]
    y_ref[...] = _rms(x2) * g_ref[...]


def _final(x1, po, mod, mod_row, final_g):
    t = x1.shape[0]
    tb = 512
    blk = pl.BlockSpec((tb, D_MODEL), lambda i: (i, 0))
    return pl.pallas_call(
        _final_kernel,
        grid=(t // tb,),
        in_specs=[blk, blk,
                  pl.BlockSpec((None, N_MOD, D_MODEL), lambda i: (mod_row(i * tb), 0, 0)),
                  pl.BlockSpec((1, D_MODEL), lambda i: (0, 0))],
        out_specs=blk,
        out_shape=_sds((t, D_MODEL), F32),
        compiler_params=_params("arbitrary"),
        name="final_norm",
    )(x1, po, mod, final_g)


def _grid_pos_embed(n_tokens):
    rows = n_tokens // GRID_W
    r = jnp.repeat(jnp.arange(rows, dtype=F32), GRID_W)
    col = jnp.tile(jnp.arange(GRID_W, dtype=F32), rows)
    quarter = D_MODEL // 4
    omega = 1.0 / (POS_THETA ** (jnp.arange(quarter, dtype=F32) / quarter))
    ar = r[:, None] * omega
    ac = col[:, None] * omega
    return jnp.concatenate([jnp.sin(ar), jnp.cos(ar), jnp.sin(ac), jnp.cos(ac)], axis=-1)


def _trunk(x, pos, mod, mod_base, per_seq, s0t, lb, w):
    n_seq, seq_len, _ = x.shape
    x2d = x.reshape(n_seq * seq_len, D_MODEL)

    def mod_row(tok):
        return mod_base + (tok // seq_len if per_seq else 0)

    z = _inproj(x2d, pos, mod, mod_row, w["norm1_g"], w["w_in"], seq_len)
    o_fw, st_fw = _hgrn(z, lb[0:1], s0t[:, 0], n_seq, seq_len, reverse=False)
    o_bw, st_bw = _hgrn(z, lb[1:2], s0t[:, 1], n_seq, seq_len, reverse=True)
    x1, hn2 = _merge(x2d, pos, z, o_fw, o_bw, mod, mod_row, seq_len, w["sgu_norm_g"], w["w_spatial"],
                     w["b_spatial_t"], w["hgrn_norm_g"], w["w_proj_a"], w["w_proj_b"], w["w_out"],
                     w["norm2_g"])
    idx_t, g_t = _topk(hn2, w["peer_w_q"], w["peer_sub_keys"])
    idx = idx_t.T
    p = _peer_up(hn2, w["peer_u"], idx)
    po = _peer_down(p, g_t.T, idx, w["peer_v"])
    y = _final(x1, po, mod, mod_row, w["final_norm_g"])
    s_fin = jnp.stack([jnp.swapaxes(st_fw, -1, -2), jnp.swapaxes(st_bw, -1, -2)], axis=1)
    return y.reshape(x.shape), s_fin


def kernel(x_prompt, x_sample, state_hgrn, c, c_ctx, w_ada, b_ada, norm1_g, w_in, sgu_norm_g,
           w_spatial, b_spatial, hgrn_lb, hgrn_norm_g, w_proj_a, w_proj_b, w_out, norm2_g,
           peer_w_q, peer_sub_keys, peer_u, peer_v, final_norm_g):
    assert w_in.shape[0] == 1, "single-layer stack"
    n_p, l_p, _ = x_prompt.shape
    n_s, l_s, _ = x_sample.shape
    assert n_s + 1 <= SUBLANES
    lb = jnp.cumsum(jax.nn.softmax(hgrn_lb.astype(F32), axis=0), axis=0)[0]

    cond = jnp.zeros((SUBLANES, D_MODEL), F32).at[0].set(c_ctx).at[1:1 + n_s].set(c)
    mod = _adaln(cond, w_ada[0], b_ada[0][None]).reshape(SUBLANES, N_MOD, D_MODEL)

    w = dict(
        norm1_g=norm1_g[0][None], w_in=w_in[0].astype(BF16), sgu_norm_g=sgu_norm_g[0][None],
        w_spatial=w_spatial[0].astype(BF16), b_spatial_t=b_spatial[0].T,
        hgrn_norm_g=hgrn_norm_g[0].reshape(1, H_B * DV),
        w_proj_a=w_proj_a[0].astype(BF16), w_proj_b=w_proj_b[0].astype(BF16),
        w_out=w_out[0].astype(BF16), norm2_g=norm2_g[0][None],
        peer_w_q=peer_w_q[0].astype(BF16),
        peer_sub_keys=peer_sub_keys[0].reshape(2 * PEER_HEADS, N_KEYS, 128).astype(BF16),
        peer_u=peer_u[0].astype(BF16), peer_v=peer_v[0].astype(BF16),
        final_norm_g=final_norm_g[None],
    )

    s0_ctx = jnp.zeros((n_p, 2, H_B, DV, DK), F32)
    s0_lat = jnp.swapaxes(state_hgrn[:, 0].astype(F32), -1, -2)
    pos = _grid_pos_embed(l_s)

    y_prompt, s_ctx = _trunk(x_prompt, None, mod, 0, False, s0_ctx, lb, w)
    y_sample, _ = _trunk(x_sample, pos, mod, 1, True, s0_lat, lb, w)
    return (y_prompt, y_sample, s_ctx[:, None])
```

```python
import functools

import jax
import jax.numpy as jnp
from jax import lax
from jax.experimental import pallas as pl
from jax.experimental.pallas import tpu as pltpu

F32 = jnp.float32
BF16 = jnp.bfloat16
I32 = jnp.int32

D_MODEL = 1024
N_MOD = 6
A_WIDTH = 512
A_GROUPS = 4
CHUNK_MLP = 128
H_B = 4
DK = 128
DV = 128
QK_WIDTH = H_B * DK
IN_WIDTH = 5632
COL_BLOCK = 512
PEER_HEADS = 8
N_KEYS = 128
PEER_TOPK = 16
PEER_Q = 2 * PEER_HEADS * 128
N_EXPERTS = N_KEYS * N_KEYS
GRID_W = 64
POS_THETA = 10000.0
EPS = 1e-6

SUBLANES = 8
LANES = 128
VMEM_LIMIT_BYTES = 56 * 1024 * 1024

TB_IN = 512
N_SPLIT_IN = 2
CH = 128
DIAG = 8
TB_MERGE = 256
TB_TOPK = 256
TB_UP = 1024
TB_DOWN = 512
E_BLOCK = 1024
A_STRIDE = 136
SCATTER_UNROLL = 32


def _sds(shape, dtype):
    return jax.ShapeDtypeStruct(shape, dtype)


def _params(*sem):
    return pltpu.CompilerParams(dimension_semantics=sem, vmem_limit_bytes=VMEM_LIMIT_BYTES)


def _dot(a, b):
    return jnp.dot(a, b, preferred_element_type=F32)


def _dot_nt(a, b):
    return lax.dot_general(a, b, (((1,), (1,)), ((), ())), preferred_element_type=F32)


def _dot_tn(a, b):
    return lax.dot_general(a, b, (((0,), (0,)), ((), ())), preferred_element_type=F32)


def _rms(x):
    return x * lax.rsqrt(jnp.mean(x * x, axis=-1, keepdims=True) + EPS)


def _adaln_kernel(c_ref, w_ref, b_ref, o_ref):
    a = jax.nn.silu(c_ref[...])
    o_ref[...] = jnp.dot(a, w_ref[...], preferred_element_type=F32,
                         precision=lax.Precision.HIGHEST) + b_ref[...]


def _adaln(cond8, w_ada, b_ada):
    n = w_ada.shape[1]
    tn = 1536
    return pl.pallas_call(
        _adaln_kernel,
        grid=(n // tn,),
        in_specs=[pl.BlockSpec((SUBLANES, D_MODEL), lambda j: (0, 0)),
                  pl.BlockSpec((D_MODEL, tn), lambda j: (0, j)),
                  pl.BlockSpec((1, tn), lambda j: (0, j))],
        out_specs=pl.BlockSpec((SUBLANES, tn), lambda j: (0, j)),
        out_shape=_sds((SUBLANES, n), F32),
        compiler_params=_params("arbitrary"),
        name="adaln",
    )(cond8, w_ada, b_ada)


def _inproj_kernel(*refs, add_pos):
    if add_pos:
        x_ref, pos_ref, mod_ref, g_ref, w_ref, z_ref, h_scr = refs
    else:
        x_ref, mod_ref, g_ref, w_ref, z_ref, h_scr = refs

    @pl.when(pl.program_id(1) == 0)
    def _():
        x = x_ref[...]
        if add_pos:
            x = x + pos_ref[...]
        h = _rms(x) * g_ref[...] * (1.0 + mod_ref[1:2, :]) + mod_ref[0:1, :]
        h_scr[...] = h.astype(BF16)

    z_ref[...] = _dot(h_scr[...], w_ref[...])


def _inproj(x2d, pos, mod, mod_row, norm_g, w_in_bf, seq_len):
    t = x2d.shape[0]
    bps = seq_len // TB_IN
    tn = IN_WIDTH // N_SPLIT_IN
    add_pos = pos is not None
    in_specs = [pl.BlockSpec((TB_IN, D_MODEL), lambda i, j: (i, 0))]
    args = [x2d]
    if add_pos:
        in_specs.append(pl.BlockSpec((TB_IN, D_MODEL), lambda i, j: (i % bps, 0)))
        args.append(pos)
    in_specs += [pl.BlockSpec((None, N_MOD, D_MODEL), lambda i, j: (mod_row(i * TB_IN), 0, 0)),
                 pl.BlockSpec((1, D_MODEL), lambda i, j: (0, 0)),
                 pl.BlockSpec((D_MODEL, tn), lambda i, j: (0, j))]
    args += [mod, norm_g, w_in_bf]
    return pl.pallas_call(
        functools.partial(_inproj_kernel, add_pos=add_pos),
        grid=(t // TB_IN, N_SPLIT_IN),
        in_specs=in_specs,
        out_specs=pl.BlockSpec((TB_IN, tn), lambda i, j: (i, j)),
        out_shape=_sds((t, IN_WIDTH), F32),
        scratch_shapes=[pltpu.VMEM((TB_IN, D_MODEL), BF16)],
        compiler_params=_params("arbitrary", "arbitrary"),
        name="inproj",
    )(*args)


def _split3(x):
    hi = x.astype(BF16)
    r1 = x - hi.astype(F32)
    mid = r1.astype(BF16)
    lo = (r1 - mid.astype(F32)).astype(BF16)
    return hi, mid, lo


def _hgrn_kernel(zq_ref, zf_ref, zv_ref, lb_ref, s0_ref, o_ref, sfin_ref,
                 st_scr, b_scr, k_scr, *, reverse, n_chunk):
    c = pl.program_id(1)

    @pl.when(c == 0)
    def _():
        st_scr[...] = s0_ref[...]

    lb = lb_ref[...]
    f = lb + (1.0 - lb) * jax.nn.sigmoid(zf_ref[...])
    k_scr[...] = 1.0 - f
    lf = jnp.log(f)
    row = lax.broadcasted_iota(I32, (CH, CH), 0)
    col = lax.broadcasted_iota(I32, (CH, CH), 1)
    tri = jnp.where((row <= col) if reverse else (row >= col), 1.0, 0.0).astype(BF16)
    hi, mid, lo = _split3(lf)
    b_scr[...] = _dot(tri, hi) + _dot(tri, mid) + _dot(tri, lo)

    lane = lax.broadcasted_iota(I32, (DIAG, LANES), 1)
    sub = lax.broadcasted_iota(I32, (DIAG, 1), 0)
    zero_tile = jnp.zeros((DIAG, DK), F32)
    n_tiles = CH // DIAG
    end_row = 0 if reverse else CH - 1

    for h in range(H_B):
        sl = slice(h * DK, (h + 1) * DK)
        tiles = []
        for j in range(n_tiles):
            r0 = j * DIAG
            qt = zq_ref[r0:r0 + DIAG, sl]
            bt = b_scr[r0:r0 + DIAG, sl]
            acc = zero_tile
            for s in range(DIAG):
                r = r0 + s
                d = jnp.minimum(bt - b_scr[r:r + 1, sl], 0.0)
                e = qt * jnp.exp(d) * k_scr[r:r + 1, sl]
                colv = jnp.sum(e, axis=-1, keepdims=True)
                colv = jnp.where((sub <= s) if reverse else (sub >= s), colv, 0.0)
                acc = jnp.where(lane == r, colv, acc)
            tiles.append(acc)
        s_tot = jnp.concatenate(tiles, axis=0)

        m = DIAG
        while m < CH:
            q_tiles, k_tiles = [], []
            for j in range(n_tiles):
                r0 = j * DIAG
                blk = r0 // m
                gives_q = (blk % 2 == 0) if reverse else (blk % 2 == 1)
                if gives_q:
                    ref_row = blk * m + m if reverse else blk * m - 1
                    qt = zq_ref[r0:r0 + DIAG, sl] * jnp.exp(
                        b_scr[r0:r0 + DIAG, sl] - b_scr[ref_row:ref_row + 1, sl])
                    q_tiles.append(qt)
                    k_tiles.append(zero_tile)
                else:
                    ref_row = blk * m if reverse else blk * m + m - 1
                    kt = k_scr[r0:r0 + DIAG, sl] * jnp.exp(
                        b_scr[ref_row:ref_row + 1, sl] - b_scr[r0:r0 + DIAG, sl])
                    k_tiles.append(kt)
                    q_tiles.append(zero_tile)
            qm = jnp.concatenate(q_tiles, axis=0).astype(BF16)
            km = jnp.concatenate(k_tiles, axis=0).astype(BF16)
            sc = _dot_nt(qm, km)
            if 2 * m < CH:
                shift = (2 * m).bit_length() - 1
                sc = jnp.where((row >> shift) == (col >> shift), sc, 0.0)
            s_tot = s_tot + sc
            m *= 2

        q = zq_ref[:, sl]
        v = zv_ref[:, sl].astype(BF16)
        bh = b_scr[:, sl]
        st = st_scr[h]
        o_h = _dot(s_tot.astype(BF16), v) + _dot_nt((q * jnp.exp(bh)).astype(BF16), st.astype(BF16))
        o_ref[:, sl] = o_h
        b_end = b_scr[end_row:end_row + 1, sl]
        ke = (k_scr[:, sl] * jnp.exp(b_end - bh)).astype(BF16)
        st_scr[h] = st * jnp.exp(b_end) + _dot_tn(v, ke)

    @pl.when(c == n_chunk - 1)
    def _():
        sfin_ref[...] = st_scr[...]


def _hgrn(z, lb_row, s0t, n_seq, seq_len, reverse):
    n_chunk = seq_len // CH
    t = n_seq * seq_len

    def rows(s, c):
        return s * n_chunk + ((n_chunk - 1 - c) if reverse else c)

    zf_col = 4 if reverse else 3
    state_spec = pl.BlockSpec((None, H_B, DV, DK), lambda s, c: (s, 0, 0, 0))
    return pl.pallas_call(
        functools.partial(_hgrn_kernel, reverse=reverse, n_chunk=n_chunk),
        grid=(n_seq, n_chunk),
        in_specs=[pl.BlockSpec((CH, COL_BLOCK), lambda s, c: (rows(s, c), 2)),
                  pl.BlockSpec((CH, COL_BLOCK), lambda s, c: (rows(s, c), zf_col)),
                  pl.BlockSpec((CH, COL_BLOCK), lambda s, c: (rows(s, c), 5)),
                  pl.BlockSpec((1, QK_WIDTH), lambda s, c: (0, 0)),
                  state_spec],
        out_specs=[pl.BlockSpec((CH, COL_BLOCK), lambda s, c: (rows(s, c), 0)), state_spec],
        out_shape=[_sds((t, H_B * DV), F32), _sds((n_seq, H_B, DV, DK), F32)],
        scratch_shapes=[pltpu.VMEM((H_B, DV, DK), F32),
                        pltpu.VMEM((CH, QK_WIDTH), F32),
                        pltpu.VMEM((CH, QK_WIDTH), F32)],
        compiler_params=_params("arbitrary", "arbitrary"),
        name="hgrn_bwd" if reverse else "hgrn_fwd",
    )(z, z, z, lb_row, s0t)


def _merge_kernel(*refs, add_pos):
    refs = list(refs)
    x_ref = refs.pop(0)
    pos_ref = refs.pop(0) if add_pos else None
    (zu_ref, zv_ref, zg_ref, za0_ref, za1_ref, zb0_ref, zb1_ref, ofw_ref, obw_ref, mod_ref,
     sgug_ref, ws_ref, bst_ref, hng_ref, wpa_ref, wpb_ref, wout_ref, n2g_ref,
     x1_ref, hn2_ref) = refs

    u = jax.nn.gelu(zu_ref[...])
    v = (_rms(jax.nn.gelu(zv_ref[...])) * sgug_ref[...]).astype(BF16)
    row_parts = []
    for ci in range(TB_MERGE // CHUNK_MLP):
        col_parts = []
        for g in range(A_GROUPS):
            vg = v[ci * CHUNK_MLP:(ci + 1) * CHUNK_MLP, g * 128:(g + 1) * 128]
            col_parts.append(_dot(ws_ref[g], vg) + bst_ref[:, g:g + 1])
        row_parts.append(jnp.concatenate(col_parts, axis=1))
    y_a = u * jnp.concatenate(row_parts, axis=0)

    o = ofw_ref[...] + obw_ref[...]
    on = jnp.concatenate([_rms(o[:, h * DV:(h + 1) * DV]) for h in range(H_B)], axis=1)
    y_b = on * hng_ref[...] * jax.nn.silu(zg_ref[...])

    pa = _dot(y_a.astype(BF16), wpa_ref[...])
    pb = _dot(y_b.astype(BF16), wpb_ref[...])
    za = jnp.concatenate([za0_ref[...], za1_ref[...]], axis=1)
    zb = jnp.concatenate([zb0_ref[...], zb1_ref[...]], axis=1)
    mix_in = jax.nn.sigmoid(za) * pa + jax.nn.sigmoid(zb) * pb
    mix = _dot(mix_in.astype(BF16), wout_ref[...])

    x = x_ref[...]
    if add_pos:
        x = x + pos_ref[...]
    x1 = x + mod_ref[2:3, :] * mix
    x1_ref[...] = x1
    hn2 = _rms(x1) * n2g_ref[...] * (1.0 + mod_ref[4:5, :]) + mod_ref[3:4, :]
    hn2_ref[...] = hn2.astype(BF16)


def _merge(x2d, pos, z, o_fw, o_bw, mod, mod_row, seq_len, sgu_g, ws_bf, bs_t, hgrn_g,
           wpa_bf, wpb_bf, wout_bf, norm2_g):
    t = x2d.shape[0]
    tb = TB_MERGE
    bps = seq_len // tb
    add_pos = pos is not None

    def zcol(k):
        return pl.BlockSpec((tb, COL_BLOCK), lambda i: (i, k))

    def full(shape):
        return pl.BlockSpec(shape, lambda i: (0,) * len(shape))

    in_specs = [pl.BlockSpec((tb, D_MODEL), lambda i: (i, 0))]
    args = [x2d]
    if add_pos:
        in_specs.append(pl.BlockSpec((tb, D_MODEL), lambda i: (i % bps, 0)))
        args.append(pos)
    in_specs += [zcol(0), zcol(1), zcol(6), zcol(7), zcol(8), zcol(9), zcol(10),
                 pl.BlockSpec((tb, A_WIDTH), lambda i: (i, 0)),
                 pl.BlockSpec((tb, A_WIDTH), lambda i: (i, 0)),
                 pl.BlockSpec((None, N_MOD, D_MODEL), lambda i: (mod_row(i * tb), 0, 0)),
                 full((1, A_WIDTH)), full((A_GROUPS, CHUNK_MLP, CHUNK_MLP)),
                 full((CHUNK_MLP, A_GROUPS)), full((1, H_B * DV)),
                 full((A_WIDTH, D_MODEL)), full((H_B * DV, D_MODEL)),
                 full((D_MODEL, D_MODEL)), full((1, D_MODEL))]
    args += [z] * 7 + [o_fw, o_bw, mod, sgu_g, ws_bf, bs_t, hgrn_g, wpa_bf, wpb_bf, wout_bf, norm2_g]
    return pl.pallas_call(
        functools.partial(_merge_kernel, add_pos=add_pos),
        grid=(t // tb,),
        in_specs=in_specs,
        out_specs=[pl.BlockSpec((tb, D_MODEL), lambda i: (i, 0)),
                   pl.BlockSpec((tb, D_MODEL), lambda i: (i, 0))],
        out_shape=[_sds((t, D_MODEL), F32), _sds((t, D_MODEL), BF16)],
        compiler_params=_params("arbitrary"),
        name="merge",
    )(*args)


def _topk_rows(s, ids, k, big, val_ref, pick):
    for r in range(k):
        m = jnp.max(s, axis=0, keepdims=True)
        first = jnp.min(jnp.where(s == m, ids, big), axis=0, keepdims=True)
        sel = ids == first
        val_ref[r:r + 1, :] = m
        pick(r, sel, first)
        s = jnp.where(sel, -jnp.inf, s)


def _topk_kernel(hn_ref, wq_ref, keys_ref, idx_ref, g_ref, v_scr, i_scr, tv_scr):
    tb = hn_ref.shape[0]
    q = _dot(hn_ref[...], wq_ref[...]).astype(BF16)
    key_ids = lax.broadcasted_iota(I32, (N_KEYS, tb), 0)
    sub8 = lax.broadcasted_iota(I32, (SUBLANES, tb), 0)

    for h in range(PEER_HEADS):
        for p in range(2):
            hp = 2 * h + p
            s = _dot_nt(keys_ref[hp], q[:, hp * 128:(hp + 1) * 128])

            def pick1(r, sel, first, p=p):
                i_scr[p, r:r + 1, :] = first

            _topk_rows(s, key_ids, PEER_TOPK, N_KEYS, v_scr.at[p], pick1)

        vals, eidx, flat = [], [], []
        for r in range(SUBLANES):
            val = v_scr[0, r:r + 1, :] + v_scr[1, 0:8, :]
            vals.append(jnp.where((r + 1) * (sub8 + 1) <= PEER_TOPK, val, -jnp.inf))
            eidx.append(i_scr[0, r:r + 1, :] * N_KEYS + i_scr[1, 0:8, :])
            flat.append(r * PEER_TOPK + sub8)
        vals.append(v_scr[0, 0:1, :] + v_scr[1, 8:16, :])
        eidx.append(i_scr[0, 0:1, :] * N_KEYS + i_scr[1, 8:16, :])
        flat.append(8 + sub8)
        vals.append(v_scr[0, 8:16, :] + v_scr[1, 0:1, :])
        eidx.append(i_scr[0, 8:16, :] * N_KEYS + i_scr[1, 0:1, :])
        flat.append((8 + sub8) * PEER_TOPK)
        cand = jnp.concatenate(vals, axis=0)
        cand_e = jnp.concatenate(eidx, axis=0)
        cand_f = jnp.concatenate(flat, axis=0)

        def pick2(j, sel, first, h=h, cand_e=cand_e):
            e = jnp.max(jnp.where(sel, cand_e, -1), axis=0, keepdims=True)
            idx_ref[h * PEER_TOPK + j:h * PEER_TOPK + j + 1, :] = e

        _topk_rows(cand, cand_f, PEER_TOPK, PEER_TOPK * PEER_TOPK, tv_scr, pick2)
        tv = tv_scr[...]
        ex = jnp.exp(tv - tv[0:1, :])
        g_ref[h * PEER_TOPK:(h + 1) * PEER_TOPK, :] = ex / jnp.sum(ex, axis=0, keepdims=True)


def _topk(hn2, wq_bf, keys_bf):
    t = hn2.shape[0]
    tb = TB_TOPK
    hk = PEER_HEADS * PEER_TOPK
    return pl.pallas_call(
        _topk_kernel,
        grid=(t // tb,),
        in_specs=[pl.BlockSpec((tb, D_MODEL), lambda i: (i, 0)),
                  pl.BlockSpec((D_MODEL, PEER_Q), lambda i: (0, 0)),
                  pl.BlockSpec((2 * PEER_HEADS, N_KEYS, 128), lambda i: (0, 0, 0))],
        out_specs=[pl.BlockSpec((hk, tb), lambda i: (0, i)),
                   pl.BlockSpec((hk, tb), lambda i: (0, i))],
        out_shape=[_sds((hk, t), I32), _sds((hk, t), F32)],
        scratch_shapes=[pltpu.VMEM((2, PEER_TOPK, tb), F32),
                        pltpu.VMEM((2, PEER_TOPK, tb), I32),
                        pltpu.VMEM((PEER_TOPK, tb), F32)],
        compiler_params=_params("arbitrary"),
        name="peer_topk",
    )(hn2, wq_bf, keys_bf)


def _peer_up_kernel(hn_ref, u_ref, idx_ref, p_ref):
    j = pl.program_id(1)

    @pl.when(j == 0)
    def _():
        p_ref[...] = jnp.zeros_like(p_ref)

    s = _dot_nt(hn_ref[...], u_ref[...])
    rc = 128
    for c in range(TB_UP // rc):
        rows = slice(c * rc, (c + 1) * rc)
        idx = idx_ref[rows, :]
        i1 = idx >> 7
        i2 = idx & (N_KEYS - 1)
        acc = p_ref[rows, :]
        for k in range(E_BLOCK // N_KEYS):
            picked = jnp.take_along_axis(s[rows, k * N_KEYS:(k + 1) * N_KEYS], i2, axis=1)
            acc = jnp.where(i1 == j * (E_BLOCK // N_KEYS) + k, picked, acc)
        p_ref[rows, :] = acc


def _peer_up(hn2, u_bf, idx):
    t = hn2.shape[0]
    hk = PEER_HEADS * PEER_TOPK
    return pl.pallas_call(
        _peer_up_kernel,
        grid=(t // TB_UP, N_EXPERTS // E_BLOCK),
        in_specs=[pl.BlockSpec((TB_UP, D_MODEL), lambda i, j: (i, 0)),
                  pl.BlockSpec((E_BLOCK, D_MODEL), lambda i, j: (j, 0)),
                  pl.BlockSpec((TB_UP, hk), lambda i, j: (i, 0))],
        out_specs=pl.BlockSpec((TB_UP, hk), lambda i, j: (i, 0)),
        out_shape=_sds((t, hk), F32),
        compiler_params=_params("arbitrary", "arbitrary"),
        name="peer_up",
    )(hn2, u_bf, idx)


def _peer_down_kernel(p_ref, g_ref, idx_ref, v_ref, o_ref, a_scr, abuf):
    j = pl.program_id(1)

    @pl.when(j == 0)
    def _():
        o_ref[...] = jnp.zeros_like(o_ref)
        a_scr[...] = jax.nn.gelu(p_ref[...]) * g_ref[...]
        key_ids = lax.broadcasted_iota(I32, (N_KEYS, PEER_HEADS * PEER_TOPK), 0)

        def body(tt, carry):
            for u in range(SCATTER_UNROLL):
                t = tt * SCATTER_UNROLL + u
                idx = idx_ref[pl.ds(t, 1), :]
                pm = jnp.where((idx >> 7) == key_ids, a_scr[pl.ds(t, 1), :], 0.0).astype(BF16)
                qm = jnp.where((idx & (N_KEYS - 1)) == key_ids, 1.0, 0.0).astype(BF16)
                abuf[pl.ds(pl.multiple_of(t * A_STRIDE, SUBLANES), N_KEYS), :] = _dot_nt(pm, qm)
            return carry

        lax.fori_loop(0, TB_DOWN // SCATTER_UNROLL, body, 0)

    per = E_BLOCK // N_KEYS
    cols = [abuf[pl.ds(j * per + k, TB_DOWN, stride=A_STRIDE), :] for k in range(per)]
    a_blk = jnp.concatenate(cols, axis=1).astype(BF16)
    o_ref[...] += _dot(a_blk, v_ref[...])


def _peer_down(p, g, idx, v_bf):
    t = p.shape[0]
    hk = PEER_HEADS * PEER_TOPK
    tok = pl.BlockSpec((TB_DOWN, hk), lambda i, j: (i, 0))
    return pl.pallas_call(
        _peer_down_kernel,
        grid=(t // TB_DOWN, N_EXPERTS // E_BLOCK),
        in_specs=[tok, tok, tok, pl.BlockSpec((E_BLOCK, D_MODEL), lambda i, j: (j, 0))],
        out_specs=pl.BlockSpec((TB_DOWN, D_MODEL), lambda i, j: (i, 0)),
        out_shape=_sds((t, D_MODEL), F32),
        scratch_shapes=[pltpu.VMEM((TB_DOWN, hk), F32),
                        pltpu.VMEM((TB_DOWN * A_STRIDE, N_KEYS), F32)],
        compiler_params=_params("arbitrary", "arbitrary"),
        name="peer_down",
    )(p, g, idx, v_bf)


def _final_kernel(x1_ref, po_ref, mod_ref, g_ref, y_ref):
    x2 = x1_ref[...] + mod_ref[5:6, :] * po_ref[...]
    y_ref[...] = _rms(x2) * g_ref[...]


def _final(x1, po, mod, mod_row, final_g):
    t = x1.shape[0]
    tb = 512
    blk = pl.BlockSpec((tb, D_MODEL), lambda i: (i, 0))
    return pl.pallas_call(
        _final_kernel,
        grid=(t // tb,),
        in_specs=[blk, blk,
                  pl.BlockSpec((None, N_MOD, D_MODEL), lambda i: (mod_row(i * tb), 0, 0)),
                  pl.BlockSpec((1, D_MODEL), lambda i: (0, 0))],
        out_specs=blk,
        out_shape=_sds((t, D_MODEL), F32),
        compiler_params=_params("arbitrary"),
        name="final_norm",
    )(x1, po, mod, final_g)


def _grid_pos_embed(n_tokens):
    rows = n_tokens // GRID_W
    r = jnp.repeat(jnp.arange(rows, dtype=F32), GRID_W)
    col = jnp.tile(jnp.arange(GRID_W, dtype=F32), rows)
    quarter = D_MODEL // 4
    omega = 1.0 / (POS_THETA ** (jnp.arange(quarter, dtype=F32) / quarter))
    ar = r[:, None] * omega
    ac = col[:, None] * omega
    return jnp.concatenate([jnp.sin(ar), jnp.cos(ar), jnp.sin(ac), jnp.cos(ac)], axis=-1)


def _trunk(x, pos, mod, mod_base, per_seq, s0t, lb, w):
    n_seq, seq_len, _ = x.shape
    x2d = x.reshape(n_seq * seq_len, D_MODEL)

    def mod_row(tok):
        return mod_base + (tok // seq_len if per_seq else 0)

    z = _inproj(x2d, pos, mod, mod_row, w["norm1_g"], w["w_in"], seq_len)
    o_fw, st_fw = _hgrn(z, lb[0:1], s0t[:, 0], n_seq, seq_len, reverse=False)
    o_bw, st_bw = _hgrn(z, lb[1:2], s0t[:, 1], n_seq, seq_len, reverse=True)
    x1, hn2 = _merge(x2d, pos, z, o_fw, o_bw, mod, mod_row, seq_len, w["sgu_norm_g"], w["w_spatial"],
                     w["b_spatial_t"], w["hgrn_norm_g"], w["w_proj_a"], w["w_proj_b"], w["w_out"],
                     w["norm2_g"])
    idx_t, g_t = _topk(hn2, w["peer_w_q"], w["peer_sub_keys"])
    idx = idx_t.T
    p = _peer_up(hn2, w["peer_u"], idx)
    po = _peer_down(p, g_t.T, idx, w["peer_v"])
    y = _final(x1, po, mod, mod_row, w["final_norm_g"])
    s_fin = jnp.stack([jnp.swapaxes(st_fw, -1, -2), jnp.swapaxes(st_bw, -1, -2)], axis=1)
    return y.reshape(x.shape), s_fin


def kernel(x_prompt, x_sample, state_hgrn, c, c_ctx, w_ada, b_ada, norm1_g, w_in, sgu_norm_g,
           w_spatial, b_spatial, hgrn_lb, hgrn_norm_g, w_proj_a, w_proj_b, w_out, norm2_g,
           peer_w_q, peer_sub_keys, peer_u, peer_v, final_norm_g):
    assert w_in.shape[0] == 1, "single-layer stack"
    n_p, l_p, _ = x_prompt.shape
    n_s, l_s, _ = x_sample.shape
    assert n_s + 1 <= SUBLANES
    lb = jnp.cumsum(jax.nn.softmax(hgrn_lb.astype(F32), axis=0), axis=0)[0]

    cond = jnp.zeros((SUBLANES, D_MODEL), F32).at[0].set(c_ctx).at[1:1 + n_s].set(c)
    mod = _adaln(cond, w_ada[0], b_ada[0][None]).reshape(SUBLANES, N_MOD, D_MODEL)

    w = dict(
        norm1_g=norm1_g[0][None], w_in=w_in[0].astype(BF16), sgu_norm_g=sgu_norm_g[0][None],
        w_spatial=w_spatial[0].astype(BF16), b_spatial_t=b_spatial[0].T,
        hgrn_norm_g=hgrn_norm_g[0].reshape(1, H_B * DV),
        w_proj_a=w_proj_a[0].astype(BF16), w_proj_b=w_proj_b[0].astype(BF16),
        w_out=w_out[0].astype(BF16), norm2_g=norm2_g[0][None],
        peer_w_q=peer_w_q[0].astype(BF16),
        peer_sub_keys=peer_sub_keys[0].reshape(2 * PEER_HEADS, N_KEYS, 128).astype(BF16),
        peer_u=peer_u[0].astype(BF16), peer_v=peer_v[0].astype(BF16),
        final_norm_g=final_norm_g[None],
    )

    s0_ctx = jnp.zeros((n_p, 2, H_B, DV, DK), F32)
    s0_lat = jnp.swapaxes(state_hgrn[:, 0].astype(F32), -1, -2)
    pos = _grid_pos_embed(l_s)

    y_prompt, s_ctx = _trunk(x_prompt, None, mod, 0, False, s0_ctx, lb, w)
    y_sample, _ = _trunk(x_sample, pos, mod, 1, True, s0_lat, lb, w)
    return (y_prompt, y_sample, s_ctx[:, None])
```

```python
import functools

import jax
import jax.numpy as jnp
import numpy as np
from jax import lax
from jax.experimental import pallas as pl
from jax.experimental.pallas import tpu as pltpu

F32 = jnp.float32
BF16 = jnp.bfloat16
I32 = jnp.int32

D_MODEL = 1024
N_MOD = 6
A_WIDTH = 512
A_GROUPS = 4
CHUNK_MLP = 128
H_B = 4
DK = 128
DV = 128
QK_WIDTH = H_B * DK
IN_WIDTH = 5632
COL_BLOCK = 512
PEER_HEADS = 8
N_KEYS = 128
PEER_TOPK = 16
PEER_Q = 2 * PEER_HEADS * 128
N_EXPERTS = N_KEYS * N_KEYS
GRID_W = 64
POS_THETA = 10000.0
EPS = 1e-6

SUBLANES = 8
LANES = 128
VMEM_LIMIT_BYTES = 56 * 1024 * 1024

TB_IN = 512
CH = 128
DIAG = 8
TB_MERGE = 256
TB_TOPK = 256
TB_UP = 1024
TB_DOWN = 512
E_BLOCK = 1024
A_STRIDE = 136
SCATTER_UNROLL = 32


def _sds(shape, dtype):
    return jax.ShapeDtypeStruct(shape, dtype)


def _params(*sem):
    return pltpu.CompilerParams(dimension_semantics=sem, vmem_limit_bytes=VMEM_LIMIT_BYTES)


def _dot(a, b):
    return jnp.dot(a, b, preferred_element_type=F32)


def _dot_nt(a, b):
    return lax.dot_general(a, b, (((1,), (1,)), ((), ())), preferred_element_type=F32)


def _dot_tn(a, b):
    return lax.dot_general(a, b, (((0,), (0,)), ((), ())), preferred_element_type=F32)


def _rms(x):
    return x * lax.rsqrt(jnp.mean(x * x, axis=-1, keepdims=True) + EPS)


def _adaln_kernel(c_ref, w_ref, b_ref, o_ref):
    a = jax.nn.silu(c_ref[...])
    o_ref[...] = jnp.dot(a, w_ref[...], preferred_element_type=F32,
                         precision=lax.Precision.HIGHEST) + b_ref[...]


def _adaln(cond8, w_ada, b_ada):
    n = w_ada.shape[1]
    tn = 1536
    return pl.pallas_call(
        _adaln_kernel,
        grid=(n // tn,),
        in_specs=[pl.BlockSpec((SUBLANES, D_MODEL), lambda j: (0, 0)),
                  pl.BlockSpec((D_MODEL, tn), lambda j: (0, j)),
                  pl.BlockSpec((1, tn), lambda j: (0, j))],
        out_specs=pl.BlockSpec((SUBLANES, tn), lambda j: (0, j)),
        out_shape=_sds((SUBLANES, n), F32),
        compiler_params=_params("arbitrary"),
        name="adaln",
    )(cond8, w_ada, b_ada)


def _inproj_kernel(*refs, add_pos):
    if add_pos:
        x_ref, pos_ref, mod_ref, g_ref, w_ref, z_ref = refs
    else:
        x_ref, mod_ref, g_ref, w_ref, z_ref = refs
    x = x_ref[...]
    if add_pos:
        x = x + pos_ref[...]
    h = _rms(x) * g_ref[...] * (1.0 + mod_ref[1:2, :]) + mod_ref[0:1, :]
    z_ref[...] = _dot(h.astype(BF16), w_ref[...])


def _inproj(x2d, pos, mod, mod_row, norm_g, w_in_bf, seq_len):
    t = x2d.shape[0]
    bps = seq_len // TB_IN
    add_pos = pos is not None
    in_specs = [pl.BlockSpec((TB_IN, D_MODEL), lambda i: (i, 0))]
    args = [x2d]
    if add_pos:
        in_specs.append(pl.BlockSpec((TB_IN, D_MODEL), lambda i: (i % bps, 0)))
        args.append(pos)
    in_specs += [pl.BlockSpec((None, N_MOD, D_MODEL), lambda i: (mod_row(i * TB_IN), 0, 0)),
                 pl.BlockSpec((1, D_MODEL), lambda i: (0, 0)),
                 pl.BlockSpec((D_MODEL, IN_WIDTH), lambda i: (0, 0), pipeline_mode=pl.Buffered(1))]
    args += [mod, norm_g, w_in_bf]
    return pl.pallas_call(
        functools.partial(_inproj_kernel, add_pos=add_pos),
        grid=(t // TB_IN,),
        in_specs=in_specs,
        out_specs=pl.BlockSpec((TB_IN, IN_WIDTH), lambda i: (i, 0)),
        out_shape=_sds((t, IN_WIDTH), F32),
        compiler_params=_params("arbitrary"),
        name="inproj",
    )(*args)


def _split3(x):
    hi = x.astype(BF16)
    r1 = x - hi.astype(F32)
    mid = r1.astype(BF16)
    lo = (r1 - mid.astype(F32)).astype(BF16)
    return hi, mid, lo


def _hgrn_kernel(zq_ref, zf_ref, zv_ref, lb_ref, s0_ref, o_ref, sfin_ref,
                 st_scr, b_scr, k_scr, *, reverse, n_chunk):
    c = pl.program_id(1)

    @pl.when(c == 0)
    def _():
        st_scr[...] = s0_ref[...]

    lb = lb_ref[...]
    f = lb + (1.0 - lb) * jax.nn.sigmoid(zf_ref[...])
    k_scr[...] = 1.0 - f
    lf = jnp.log(f)
    row = lax.broadcasted_iota(I32, (CH, CH), 0)
    col = lax.broadcasted_iota(I32, (CH, CH), 1)
    tri = jnp.where((row <= col) if reverse else (row >= col), 1.0, 0.0).astype(BF16)
    hi, mid, lo = _split3(lf)
    b_scr[...] = _dot(tri, hi) + _dot(tri, mid) + _dot(tri, lo)

    lane = lax.broadcasted_iota(I32, (DIAG, LANES), 1)
    sub = lax.broadcasted_iota(I32, (DIAG, 1), 0)
    zero_tile = jnp.zeros((DIAG, DK), F32)
    n_tiles = CH // DIAG
    end_row = 0 if reverse else CH - 1

    for h in range(H_B):
        sl = slice(h * DK, (h + 1) * DK)
        tiles = []
        for j in range(n_tiles):
            r0 = j * DIAG
            qt = zq_ref[r0:r0 + DIAG, sl]
            bt = b_scr[r0:r0 + DIAG, sl]
            acc = zero_tile
            for s in range(DIAG):
                r = r0 + s
                d = jnp.minimum(bt - b_scr[r:r + 1, sl], 0.0)
                e = qt * jnp.exp(d) * k_scr[r:r + 1, sl]
                colv = jnp.sum(e, axis=-1, keepdims=True)
                colv = jnp.where((sub <= s) if reverse else (sub >= s), colv, 0.0)
                acc = jnp.where(lane == r, colv, acc)
            tiles.append(acc)
        s_tot = jnp.concatenate(tiles, axis=0)

        m = DIAG
        while m < CH:
            q_tiles, k_tiles = [], []
            for j in range(n_tiles):
                r0 = j * DIAG
                blk = r0 // m
                gives_q = (blk % 2 == 0) if reverse else (blk % 2 == 1)
                if gives_q:
                    ref_row = blk * m + m if reverse else blk * m - 1
                    qt = zq_ref[r0:r0 + DIAG, sl] * jnp.exp(
                        b_scr[r0:r0 + DIAG, sl] - b_scr[ref_row:ref_row + 1, sl])
                    q_tiles.append(qt)
                    k_tiles.append(zero_tile)
                else:
                    ref_row = blk * m if reverse else blk * m + m - 1
                    kt = k_scr[r0:r0 + DIAG, sl] * jnp.exp(
                        b_scr[ref_row:ref_row + 1, sl] - b_scr[r0:r0 + DIAG, sl])
                    k_tiles.append(kt)
                    q_tiles.append(zero_tile)
            qm = jnp.concatenate(q_tiles, axis=0).astype(BF16)
            km = jnp.concatenate(k_tiles, axis=0).astype(BF16)
            sc = _dot_nt(qm, km)
            if 2 * m < CH:
                shift = (2 * m).bit_length() - 1
                sc = jnp.where((row >> shift) == (col >> shift), sc, 0.0)
            s_tot = s_tot + sc
            m *= 2

        q = zq_ref[:, sl]
        v = zv_ref[:, sl].astype(BF16)
        bh = b_scr[:, sl]
        st = st_scr[h]
        o_h = _dot(s_tot.astype(BF16), v) + _dot_nt((q * jnp.exp(bh)).astype(BF16), st.astype(BF16))
        o_ref[:, sl] = o_h
        b_end = b_scr[end_row:end_row + 1, sl]
        ke = (k_scr[:, sl] * jnp.exp(b_end - bh)).astype(BF16)
        st_scr[h] = st * jnp.exp(b_end) + _dot_tn(v, ke)

    @pl.when(c == n_chunk - 1)
    def _():
        sfin_ref[...] = st_scr[...]


def _hgrn(z, lb_row, s0t, n_seq, seq_len, reverse):
    n_chunk = seq_len // CH
    t = n_seq * seq_len

    def rows(s, c):
        return s * n_chunk + ((n_chunk - 1 - c) if reverse else c)

    zf_col = 4 if reverse else 3
    state_spec = pl.BlockSpec((None, H_B, DV, DK), lambda s, c: (s, 0, 0, 0))
    return pl.pallas_call(
        functools.partial(_hgrn_kernel, reverse=reverse, n_chunk=n_chunk),
        grid=(n_seq, n_chunk),
        in_specs=[pl.BlockSpec((CH, COL_BLOCK), lambda s, c: (rows(s, c), 2)),
                  pl.BlockSpec((CH, COL_BLOCK), lambda s, c: (rows(s, c), zf_col)),
                  pl.BlockSpec((CH, COL_BLOCK), lambda s, c: (rows(s, c), 5)),
                  pl.BlockSpec((1, QK_WIDTH), lambda s, c: (0, 0)),
                  state_spec],
        out_specs=[pl.BlockSpec((CH, COL_BLOCK), lambda s, c: (rows(s, c), 0)), state_spec],
        out_shape=[_sds((t, H_B * DV), F32), _sds((n_seq, H_B, DV, DK), F32)],
        scratch_shapes=[pltpu.VMEM((H_B, DV, DK), F32),
                        pltpu.VMEM((CH, QK_WIDTH), F32),
                        pltpu.VMEM((CH, QK_WIDTH), F32)],
        compiler_params=_params("arbitrary", "arbitrary"),
        name="hgrn_bwd" if reverse else "hgrn_fwd",
    )(z, z, z, lb_row, s0t)


def _merge_kernel(*refs, add_pos):
    refs = list(refs)
    x_ref = refs.pop(0)
    pos_ref = refs.pop(0) if add_pos else None
    (zu_ref, zv_ref, zg_ref, za0_ref, za1_ref, zb0_ref, zb1_ref, ofw_ref, obw_ref, mod_ref,
     sgug_ref, ws_ref, bst_ref, hng_ref, wpa_ref, wpb_ref, wout_ref, n2g_ref,
     x1_ref, hn2_ref) = refs

    u = jax.nn.gelu(zu_ref[...])
    v = (_rms(jax.nn.gelu(zv_ref[...])) * sgug_ref[...]).astype(BF16)
    row_parts = []
    for ci in range(TB_MERGE // CHUNK_MLP):
        col_parts = []
        for g in range(A_GROUPS):
            vg = v[ci * CHUNK_MLP:(ci + 1) * CHUNK_MLP, g * 128:(g + 1) * 128]
            col_parts.append(_dot(ws_ref[g], vg) + bst_ref[:, g:g + 1])
        row_parts.append(jnp.concatenate(col_parts, axis=1))
    y_a = u * jnp.concatenate(row_parts, axis=0)

    o = ofw_ref[...] + obw_ref[...]
    on = jnp.concatenate([_rms(o[:, h * DV:(h + 1) * DV]) for h in range(H_B)], axis=1)
    y_b = on * hng_ref[...] * jax.nn.silu(zg_ref[...])

    pa = _dot(y_a.astype(BF16), wpa_ref[...])
    pb = _dot(y_b.astype(BF16), wpb_ref[...])
    za = jnp.concatenate([za0_ref[...], za1_ref[...]], axis=1)
    zb = jnp.concatenate([zb0_ref[...], zb1_ref[...]], axis=1)
    mix_in = jax.nn.sigmoid(za) * pa + jax.nn.sigmoid(zb) * pb
    mix = _dot(mix_in.astype(BF16), wout_ref[...])

    x = x_ref[...]
    if add_pos:
        x = x + pos_ref[...]
    x1 = x + mod_ref[2:3, :] * mix
    x1_ref[...] = x1
    hn2 = _rms(x1) * n2g_ref[...] * (1.0 + mod_ref[4:5, :]) + mod_ref[3:4, :]
    hn2_ref[...] = hn2.astype(BF16)


def _merge(x2d, pos, z, o_fw, o_bw, mod, mod_row, seq_len, sgu_g, ws_bf, bs_t, hgrn_g,
           wpa_bf, wpb_bf, wout_bf, norm2_g):
    t = x2d.shape[0]
    tb = TB_MERGE
    bps = seq_len // tb
    add_pos = pos is not None

    def zcol(k):
        return pl.BlockSpec((tb, COL_BLOCK), lambda i: (i, k))

    def full(shape):
        return pl.BlockSpec(shape, lambda i: (0,) * len(shape))

    in_specs = [pl.BlockSpec((tb, D_MODEL), lambda i: (i, 0))]
    args = [x2d]
    if add_pos:
        in_specs.append(pl.BlockSpec((tb, D_MODEL), lambda i: (i % bps, 0)))
        args.append(pos)
    in_specs += [zcol(0), zcol(1), zcol(6), zcol(7), zcol(8), zcol(9), zcol(10),
                 pl.BlockSpec((tb, A_WIDTH), lambda i: (i, 0)),
                 pl.BlockSpec((tb, A_WIDTH), lambda i: (i, 0)),
                 pl.BlockSpec((None, N_MOD, D_MODEL), lambda i: (mod_row(i * tb), 0, 0)),
                 full((1, A_WIDTH)), full((A_GROUPS, CHUNK_MLP, CHUNK_MLP)),
                 full((CHUNK_MLP, A_GROUPS)), full((1, H_B * DV)),
                 full((A_WIDTH, D_MODEL)), full((H_B * DV, D_MODEL)),
                 full((D_MODEL, D_MODEL)), full((1, D_MODEL))]
    args += [z] * 7 + [o_fw, o_bw, mod, sgu_g, ws_bf, bs_t, hgrn_g, wpa_bf, wpb_bf, wout_bf, norm2_g]
    return pl.pallas_call(
        functools.partial(_merge_kernel, add_pos=add_pos),
        grid=(t // tb,),
        in_specs=in_specs,
        out_specs=[pl.BlockSpec((tb, D_MODEL), lambda i: (i, 0)),
                   pl.BlockSpec((tb, D_MODEL), lambda i: (i, 0))],
        out_shape=[_sds((t, D_MODEL), F32), _sds((t, D_MODEL), BF16)],
        compiler_params=_params("arbitrary"),
        name="merge",
    )(*args)


def _sort_network(n):
    def merge(lo, hi, r):
        step = r * 2
        if step < hi - lo:
            yield from merge(lo, hi, step)
            yield from merge(lo + r, hi, step)
            yield from [(i, i + r) for i in range(lo + r, hi - r, step)]
        else:
            yield (lo, lo + r)

    def sort(lo, hi):
        if hi - lo >= 1:
            mid = lo + (hi - lo) // 2
            yield from sort(lo, mid)
            yield from sort(mid + 1, hi)
            yield from merge(lo, hi, 1)

    return tuple(sort(0, n - 1))


def _col_max(x):
    return jnp.max(x, axis=0, keepdims=True)


def _col_min(x):
    return jnp.min(x, axis=0, keepdims=True)


def _topk_kernel(hn_ref, wq_ref, keys_ref, idx_ref, g_ref, v_scr, i_scr, tv_scr, fp_scr):
    tb = hn_ref.shape[0]
    k = PEER_TOPK
    q = _dot(hn_ref[...], wq_ref[...]).astype(BF16)
    sub = lax.broadcasted_iota(I32, (SUBLANES, tb), 0).astype(F32)
    n_tiles = N_KEYS // SUBLANES
    neg = -jnp.inf

    for h in range(PEER_HEADS):
        for p in range(2):
            hp = 2 * h + p
            s = _dot_nt(keys_ref[hp], q[:, hp * 128:(hp + 1) * 128])
            vals = [s[SUBLANES * v:SUBLANES * (v + 1), :] for v in range(n_tiles)]
            ids = [sub + float(SUBLANES * v) for v in range(n_tiles)]
            for i, j in _sort_network(n_tiles):
                a, b, ia, ib = vals[i], vals[j], ids[i], ids[j]
                swap = (b > a) | ((b == a) & (ib < ia))
                vals[i], vals[j] = jnp.where(swap, b, a), jnp.where(swap, a, b)
                ids[i], ids[j] = jnp.where(swap, ib, ia), jnp.where(swap, ia, ib)
            for r in range(k):
                m = _col_max(vals[0])
                first = _col_min(jnp.where(vals[0] == m, ids[0], float(N_KEYS)))
                sel = ids[0] == first
                v_scr[p, r:r + 1, :] = m
                i_scr[p, r:r + 1, :] = first.astype(I32)
                left = k - 1 - r
                for d in range(left):
                    vals[d] = jnp.where(sel, vals[d + 1], vals[d])
                    ids[d] = jnp.where(sel, ids[d + 1], ids[d])

        lists = []
        for r in range(k):
            val = v_scr[0, r:r + 1, :] + v_scr[1, 0:SUBLANES, :]
            n_valid = k // (r + 1)
            lists.append(val if n_valid >= SUBLANES else jnp.where(sub < float(n_valid), val, neg))
        extra = v_scr[0, 0:1, :] + v_scr[1, SUBLANES:k, :]
        flat = sub
        flat_extra = sub + float(SUBLANES)
        for j in range(k):
            m = _col_max(jnp.maximum(lists[0], extra))
            first = _col_min(jnp.minimum(jnp.where(lists[0] == m, flat, float(k * k)),
                                         jnp.where(extra == m, flat_extra, float(k * k))))
            tv_scr[j:j + 1, :] = m
            fp_scr[j:j + 1, :] = first.astype(I32)
            left = k - 1 - j
            if left:
                sel = flat == first
                for d in range(left):
                    lists[d] = jnp.where(sel, lists[d + 1], lists[d])
                flat = jnp.where(sel, flat + float(k), flat)
                extra = jnp.where(flat_extra == first, neg, extra)

        fp = fp_scr[...]
        pr = fp >> 4
        pc = fp & (k - 1)
        e1 = jnp.zeros_like(fp)
        e2 = jnp.zeros_like(fp)
        for r in range(k):
            e1 = jnp.where(pr == r, i_scr[0, r:r + 1, :], e1)
            e2 = jnp.where(pc == r, i_scr[1, r:r + 1, :], e2)
        idx_ref[h * k:(h + 1) * k, :] = e1 * N_KEYS + e2
        tv = tv_scr[...]
        ex = jnp.exp(tv - tv[0:1, :])
        g_ref[h * k:(h + 1) * k, :] = ex / jnp.sum(ex, axis=0, keepdims=True)


def _topk(hn2, wq_bf, keys_bf):
    t = hn2.shape[0]
    tb = TB_TOPK
    hk = PEER_HEADS * PEER_TOPK
    return pl.pallas_call(
        _topk_kernel,
        grid=(t // tb,),
        in_specs=[pl.BlockSpec((tb, D_MODEL), lambda i: (i, 0)),
                  pl.BlockSpec((D_MODEL, PEER_Q), lambda i: (0, 0)),
                  pl.BlockSpec((2 * PEER_HEADS, N_KEYS, 128), lambda i: (0, 0, 0))],
        out_specs=[pl.BlockSpec((hk, tb), lambda i: (0, i)),
                   pl.BlockSpec((hk, tb), lambda i: (0, i))],
        out_shape=[_sds((hk, t), I32), _sds((hk, t), F32)],
        scratch_shapes=[pltpu.VMEM((2, PEER_TOPK, tb), F32),
                        pltpu.VMEM((2, PEER_TOPK, tb), I32),
                        pltpu.VMEM((PEER_TOPK, tb), F32),
                        pltpu.VMEM((PEER_TOPK, tb), I32)],
        compiler_params=_params("arbitrary"),
        name="peer_topk",
    )(hn2, wq_bf, keys_bf)


def _peer_up_kernel(hn_ref, u_ref, idx_ref, p_ref, s_even, s_odd):
    j = pl.program_id(1)
    per = E_BLOCK // N_KEYS

    @pl.when(j == 0)
    def _():
        p_ref[...] = jnp.zeros_like(p_ref)
        s_odd[...] = jnp.zeros_like(s_odd)

    def step(cur, prev):
        cur[...] = _dot_nt(hn_ref[...], u_ref[...].astype(BF16))
        first_row = (j - 1) * per
        for c in range(TB_UP // SUBLANES):
            rows = slice(c * SUBLANES, (c + 1) * SUBLANES)
            idx = idx_ref[rows, :]
            i1 = idx >> 7
            i2 = idx & (N_KEYS - 1)
            acc = p_ref[rows, :]
            for k in range(per):
                picked = jnp.take_along_axis(prev[rows, k * N_KEYS:(k + 1) * N_KEYS], i2, axis=1)
                acc = jnp.where(i1 == first_row + k, picked, acc)
            p_ref[rows, :] = acc

    @pl.when(j % 2 == 0)
    def _():
        step(s_even, s_odd)

    @pl.when(j % 2 == 1)
    def _():
        step(s_odd, s_even)


def _peer_up(hn2, u_bf, idx):
    t = hn2.shape[0]
    hk = PEER_HEADS * PEER_TOPK
    n_blocks = N_EXPERTS // E_BLOCK
    return pl.pallas_call(
        _peer_up_kernel,
        grid=(t // TB_UP, n_blocks + 1),
        in_specs=[pl.BlockSpec((TB_UP, D_MODEL), lambda i, j: (i, 0)),
                  pl.BlockSpec((E_BLOCK, D_MODEL), lambda i, j: (jnp.minimum(j, n_blocks - 1), 0)),
                  pl.BlockSpec((TB_UP, hk), lambda i, j: (i, 0))],
        out_specs=pl.BlockSpec((TB_UP, hk), lambda i, j: (i, 0)),
        out_shape=_sds((t, hk), F32),
        scratch_shapes=[pltpu.VMEM((TB_UP, E_BLOCK), F32), pltpu.VMEM((TB_UP, E_BLOCK), F32)],
        compiler_params=_params("arbitrary", "arbitrary"),
        name="peer_up",
    )(hn2, u_bf, idx)


def _peer_down_kernel(p_ref, g_ref, idx_ref, v_ref, x1_ref, mod_ref, fg_ref, y_ref, a_scr, abuf):
    j = pl.program_id(1)

    @pl.when(j == 0)
    def _():
        y_ref[...] = jnp.zeros_like(y_ref)
        a_scr[...] = jax.nn.gelu(p_ref[...]) * g_ref[...]
        key_ids = lax.broadcasted_iota(I32, (N_KEYS, PEER_HEADS * PEER_TOPK), 0)

        def body(tt, carry):
            for u in range(SCATTER_UNROLL):
                t = tt * SCATTER_UNROLL + u
                idx = idx_ref[pl.ds(t, 1), :]
                pm = jnp.where((idx >> 7) == key_ids, a_scr[pl.ds(t, 1), :], 0.0).astype(BF16)
                qm = jnp.where((idx & (N_KEYS - 1)) == key_ids, 1.0, 0.0).astype(BF16)
                abuf[pl.ds(pl.multiple_of(t * A_STRIDE, SUBLANES), N_KEYS), :] = _dot_nt(pm, qm)
            return carry

        lax.fori_loop(0, TB_DOWN // SCATTER_UNROLL, body, 0)

    per = E_BLOCK // N_KEYS
    cols = [abuf[pl.ds(j * per + k, TB_DOWN, stride=A_STRIDE), :] for k in range(per)]
    a_blk = jnp.concatenate(cols, axis=1).astype(BF16)
    y_ref[...] += _dot(a_blk, v_ref[...])

    @pl.when(j == pl.num_programs(1) - 1)
    def _():
        x2 = x1_ref[...] + mod_ref[5:6, :] * y_ref[...]
        y_ref[...] = _rms(x2) * fg_ref[...]


def _peer_down(p, g, idx, v_bf, x1, mod, mod_row, final_g):
    t = p.shape[0]
    hk = PEER_HEADS * PEER_TOPK
    tok = pl.BlockSpec((TB_DOWN, hk), lambda i, j: (i, 0))
    wide = pl.BlockSpec((TB_DOWN, D_MODEL), lambda i, j: (i, 0))
    return pl.pallas_call(
        _peer_down_kernel,
        grid=(t // TB_DOWN, N_EXPERTS // E_BLOCK),
        in_specs=[tok, tok, tok, pl.BlockSpec((E_BLOCK, D_MODEL), lambda i, j: (j, 0)), wide,
                  pl.BlockSpec((None, N_MOD, D_MODEL), lambda i, j: (mod_row(i * TB_DOWN), 0, 0)),
                  pl.BlockSpec((1, D_MODEL), lambda i, j: (0, 0))],
        out_specs=wide,
        out_shape=_sds((t, D_MODEL), F32),
        scratch_shapes=[pltpu.VMEM((TB_DOWN, hk), F32),
                        pltpu.VMEM((TB_DOWN * A_STRIDE, N_KEYS), F32)],
        compiler_params=_params("arbitrary", "arbitrary"),
        name="peer_down",
    )(p, g, idx, v_bf, x1, mod, final_g)


def _grid_pos_embed(n_tokens):
    rows = n_tokens // GRID_W
    r = np.repeat(np.arange(rows, dtype=np.float64), GRID_W)
    col = np.tile(np.arange(GRID_W, dtype=np.float64), rows)
    quarter = D_MODEL // 4
    omega = 1.0 / (POS_THETA ** (np.arange(quarter, dtype=np.float64) / quarter))
    ar = r[:, None] * omega
    ac = col[:, None] * omega
    table = np.concatenate([np.sin(ar), np.cos(ar), np.sin(ac), np.cos(ac)], axis=-1)
    return jnp.asarray(table.astype(np.float32))


def _trunk(x, pos, mod, mod_base, per_seq, s0t, lb, w):
    n_seq, seq_len, _ = x.shape
    x2d = x.reshape(n_seq * seq_len, D_MODEL)

    def mod_row(tok):
        return mod_base + (tok // seq_len if per_seq else 0)

    z = _inproj(x2d, pos, mod, mod_row, w["norm1_g"], w["w_in"], seq_len)
    o_fw, st_fw = _hgrn(z, lb[0:1], s0t[:, 0], n_seq, seq_len, reverse=False)
    o_bw, st_bw = _hgrn(z, lb[1:2], s0t[:, 1], n_seq, seq_len, reverse=True)
    x1, hn2 = _merge(x2d, pos, z, o_fw, o_bw, mod, mod_row, seq_len, w["sgu_norm_g"], w["w_spatial"],
                     w["b_spatial_t"], w["hgrn_norm_g"], w["w_proj_a"], w["w_proj_b"], w["w_out"],
                     w["norm2_g"])
    idx_t, g_t = _topk(hn2, w["peer_w_q"], w["peer_sub_keys"])
    idx = idx_t.T
    p = _peer_up(hn2, w["peer_u"], idx)
    y = _peer_down(p, g_t.T, idx, w["peer_v"], x1, mod, mod_row, w["final_norm_g"])
    s_fin = jnp.stack([jnp.swapaxes(st_fw, -1, -2), jnp.swapaxes(st_bw, -1, -2)], axis=1)
    return y.reshape(x.shape), s_fin


def kernel(x_prompt, x_sample, state_hgrn, c, c_ctx, w_ada, b_ada, norm1_g, w_in, sgu_norm_g,
           w_spatial, b_spatial, hgrn_lb, hgrn_norm_g, w_proj_a, w_proj_b, w_out, norm2_g,
           peer_w_q, peer_sub_keys, peer_u, peer_v, final_norm_g):
    assert w_in.shape[0] == 1, "single-layer stack"
    n_p, l_p, _ = x_prompt.shape
    n_s, l_s, _ = x_sample.shape
    assert n_s + 1 <= SUBLANES
    lb = jnp.cumsum(jax.nn.softmax(hgrn_lb.astype(F32), axis=0), axis=0)[0]

    cond = jnp.zeros((SUBLANES, D_MODEL), F32).at[0].set(c_ctx).at[1:1 + n_s].set(c)
    mod = _adaln(cond, w_ada[0], b_ada[0][None]).reshape(SUBLANES, N_MOD, D_MODEL)

    w = dict(
        norm1_g=norm1_g[0][None], w_in=w_in[0].astype(BF16), sgu_norm_g=sgu_norm_g[0][None],
        w_spatial=w_spatial[0].astype(BF16), b_spatial_t=b_spatial[0].T,
        hgrn_norm_g=hgrn_norm_g[0].reshape(1, H_B * DV),
        w_proj_a=w_proj_a[0].astype(BF16), w_proj_b=w_proj_b[0].astype(BF16),
        w_out=w_out[0].astype(BF16), norm2_g=norm2_g[0][None],
        peer_w_q=peer_w_q[0].astype(BF16),
        peer_sub_keys=peer_sub_keys[0].reshape(2 * PEER_HEADS, N_KEYS, 128).astype(BF16),
        peer_u=peer_u[0], peer_v=peer_v[0].astype(BF16),
        final_norm_g=final_norm_g[None],
    )

    s0_ctx = jnp.zeros((n_p, 2, H_B, DV, DK), F32)
    s0_lat = jnp.swapaxes(state_hgrn[:, 0].astype(F32), -1, -2)
    pos = _grid_pos_embed(l_s)

    y_prompt, s_ctx = _trunk(x_prompt, None, mod, 0, False, s0_ctx, lb, w)
    y_sample, _ = _trunk(x_sample, pos, mod, 1, True, s0_lat, lb, w)
    return (y_prompt, y_sample, s_ctx[:, None])
```

```python
import functools

import jax
import jax.numpy as jnp
import numpy as np
from jax import lax
from jax.experimental import pallas as pl
from jax.experimental.pallas import tpu as pltpu

F32 = jnp.float32
BF16 = jnp.bfloat16
I32 = jnp.int32

D_MODEL = 1024
N_MOD = 6
A_WIDTH = 512
A_GROUPS = 4
CHUNK_MLP = 128
H_B = 4
DK = 128
DV = 128
QK_WIDTH = H_B * DK
IN_WIDTH = 5632
COL_BLOCK = 512
PEER_HEADS = 8
N_KEYS = 128
PEER_TOPK = 16
PEER_Q = 2 * PEER_HEADS * 128
N_EXPERTS = N_KEYS * N_KEYS
GRID_W = 64
POS_THETA = 10000.0
EPS = 1e-6

SUBLANES = 8
LANES = 128
VMEM_LIMIT_BYTES = 56 * 1024 * 1024

TB_IN = 512
CH = 128
TB_MERGE = 256
TB_TOPK = 256
TB_UP = 1024
TB_DOWN = 512
E_BLOCK = 1024
A_STRIDE = 136
SCATTER_UNROLL = 32
STREAM_AHEAD = 2
STREAM_SLOTS = STREAM_AHEAD + 1


def _sds(shape, dtype):
    return jax.ShapeDtypeStruct(shape, dtype)


def _params(*sem):
    return pltpu.CompilerParams(dimension_semantics=sem, vmem_limit_bytes=VMEM_LIMIT_BYTES)


def _dot(a, b):
    return jnp.dot(a, b, preferred_element_type=F32)


def _dot_nt(a, b):
    return lax.dot_general(a, b, (((1,), (1,)), ((), ())), preferred_element_type=F32)


def _dot_tn(a, b):
    return lax.dot_general(a, b, (((0,), (0,)), ((), ())), preferred_element_type=F32)


def _rms(x):
    return x * lax.rsqrt(jnp.mean(x * x, axis=-1, keepdims=True) + EPS)


def _adaln_kernel(c_ref, w_ref, b_ref, o_ref):
    a = jax.nn.silu(c_ref[...])
    o_ref[...] = jnp.dot(a, w_ref[...], preferred_element_type=F32,
                         precision=lax.Precision.HIGHEST) + b_ref[...]


def _adaln(cond8, w_ada, b_ada):
    n = w_ada.shape[1]
    tn = 1536
    return pl.pallas_call(
        _adaln_kernel,
        grid=(n // tn,),
        in_specs=[pl.BlockSpec((SUBLANES, D_MODEL), lambda j: (0, 0)),
                  pl.BlockSpec((D_MODEL, tn), lambda j: (0, j)),
                  pl.BlockSpec((1, tn), lambda j: (0, j))],
        out_specs=pl.BlockSpec((SUBLANES, tn), lambda j: (0, j)),
        out_shape=_sds((SUBLANES, n), F32),
        compiler_params=_params("arbitrary"),
        name="adaln",
    )(cond8, w_ada, b_ada)


def _inproj_kernel(*refs, add_pos):
    if add_pos:
        x_ref, pos_ref, mod_ref, g_ref, w_ref, z_ref = refs
    else:
        x_ref, mod_ref, g_ref, w_ref, z_ref = refs
    x = x_ref[...]
    if add_pos:
        x = x + pos_ref[...]
    h = _rms(x) * g_ref[...] * (1.0 + mod_ref[1:2, :]) + mod_ref[0:1, :]
    z_ref[...] = _dot(h.astype(BF16), w_ref[...])


def _inproj(x2d, pos, mod, mod_row, norm_g, w_in_bf, seq_len):
    t = x2d.shape[0]
    bps = seq_len // TB_IN
    add_pos = pos is not None
    in_specs = [pl.BlockSpec((TB_IN, D_MODEL), lambda i: (i, 0))]
    args = [x2d]
    if add_pos:
        in_specs.append(pl.BlockSpec((TB_IN, D_MODEL), lambda i: (i % bps, 0)))
        args.append(pos)
    in_specs += [pl.BlockSpec((None, N_MOD, D_MODEL), lambda i: (mod_row(i * TB_IN), 0, 0)),
                 pl.BlockSpec((1, D_MODEL), lambda i: (0, 0)),
                 pl.BlockSpec((D_MODEL, IN_WIDTH), lambda i: (0, 0), pipeline_mode=pl.Buffered(1))]
    args += [mod, norm_g, w_in_bf]
    return pl.pallas_call(
        functools.partial(_inproj_kernel, add_pos=add_pos),
        grid=(t // TB_IN,),
        in_specs=in_specs,
        out_specs=pl.BlockSpec((TB_IN, IN_WIDTH), lambda i: (i, 0)),
        out_shape=_sds((t, IN_WIDTH), F32),
        compiler_params=_params("arbitrary"),
        name="inproj",
    )(*args)


def _split3(x):
    hi = x.astype(BF16)
    r1 = x - hi.astype(F32)
    mid = r1.astype(BF16)
    lo = (r1 - mid.astype(F32)).astype(BF16)
    return hi, mid, lo


def _hgrn_direction(zq_ref, zf_ref, zv_ref, lb, st_scr, b_scr, k_scr, o_ref, reverse):
    f = lb + (1.0 - lb) * jax.nn.sigmoid(zf_ref[...])
    k_scr[...] = 1.0 - f
    lf = jnp.log(f)
    row = lax.broadcasted_iota(I32, (CH, CH), 0)
    col = lax.broadcasted_iota(I32, (CH, CH), 1)
    tri = jnp.where((row <= col) if reverse else (row >= col), 1.0, 0.0).astype(BF16)
    hi, mid, lo = _split3(lf)
    b_scr[...] = _dot(tri, hi) + _dot(tri, mid) + _dot(tri, lo)

    sub = lax.broadcasted_iota(I32, (SUBLANES, DK), 0)
    zero_tile = jnp.zeros((SUBLANES, DK), F32)
    n_tiles = CH // SUBLANES
    end_row = 0 if reverse else CH - 1

    for h in range(H_B):
        sl = slice(h * DK, (h + 1) * DK)
        q = zq_ref[:, sl]
        kk = k_scr[:, sl]
        bh = b_scr[:, sl]

        def masked(sc, m):
            if 2 * m >= CH:
                return sc
            shift = (2 * m).bit_length() - 1 if m else 0
            return jnp.where((row >> shift) == (col >> shift), sc, 0.0)

        s_tot = masked(_dot_nt(q.astype(BF16), kk.astype(BF16)), 0)

        m = 1
        while m < CH:
            q_tiles, k_tiles = [], []
            for j in range(n_tiles):
                r0 = j * SUBLANES
                rows = slice(r0, r0 + SUBLANES)
                if m < SUBLANES:
                    upper = (sub & m) != 0
                    gives_q = ~upper if reverse else upper
                    if m == 1:
                        scale_q = 1.0 - k_scr[rows, sl]
                        scale_k = None
                    else:
                        seam = m if reverse else m - 1
                        if 2 * m == SUBLANES:
                            b_mid = b_scr[r0 + seam:r0 + seam + 1, sl]
                        else:
                            b_mid = jnp.where(sub < 2 * m, b_scr[r0 + seam:r0 + seam + 1, sl],
                                              b_scr[r0 + 2 * m + seam:r0 + 2 * m + seam + 1, sl])
                        d = b_scr[rows, sl] - b_mid
                        scale_q = scale_k = jnp.exp(jnp.minimum(d, -d))
                    qt = zq_ref[rows, sl] * scale_q
                    kt = k_scr[rows, sl] if scale_k is None else k_scr[rows, sl] * scale_k
                    q_tiles.append(jnp.where(gives_q, qt, 0.0))
                    k_tiles.append(jnp.where(gives_q, 0.0, kt))
                else:
                    blk = r0 // m
                    gives_q = (blk % 2 == 0) if reverse else (blk % 2 == 1)
                    if gives_q:
                        ref_row = blk * m + m if reverse else blk * m - 1
                        q_tiles.append(zq_ref[rows, sl] * jnp.exp(
                            b_scr[rows, sl] - b_scr[ref_row:ref_row + 1, sl]))
                        k_tiles.append(zero_tile)
                    else:
                        ref_row = blk * m if reverse else blk * m + m - 1
                        k_tiles.append(k_scr[rows, sl] * jnp.exp(
                            b_scr[ref_row:ref_row + 1, sl] - b_scr[rows, sl]))
                        q_tiles.append(zero_tile)
            qm = jnp.concatenate(q_tiles, axis=0).astype(BF16)
            km = jnp.concatenate(k_tiles, axis=0).astype(BF16)
            s_tot = s_tot + masked(_dot_nt(qm, km), m)
            m *= 2

        v = zv_ref[:, sl].astype(BF16)
        st = st_scr[h]
        o_ref[:, sl] = (_dot(s_tot.astype(BF16), v)
                        + _dot_nt((q * jnp.exp(bh)).astype(BF16), st.astype(BF16)))
        b_end = b_scr[end_row:end_row + 1, sl]
        ke = (kk * jnp.exp(b_end - bh)).astype(BF16)
        st_scr[h] = st * jnp.exp(b_end) + _dot_tn(v, ke)


def _hgrn_kernel(zqf_ref, zff_ref, zvf_ref, zqb_ref, zfb_ref, zvb_ref, lb_ref, s0_ref,
                 ofw_ref, obw_ref, sfin_ref, st_scr, b_scr, k_scr, *, n_chunk):
    c = pl.program_id(1)

    @pl.when(c == 0)
    def _():
        st_scr[...] = s0_ref[...]

    _hgrn_direction(zqf_ref, zff_ref, zvf_ref, lb_ref[0:1, :], st_scr.at[0], b_scr.at[0],
                    k_scr.at[0], ofw_ref, reverse=False)
    _hgrn_direction(zqb_ref, zfb_ref, zvb_ref, lb_ref[1:2, :], st_scr.at[1], b_scr.at[1],
                    k_scr.at[1], obw_ref, reverse=True)

    @pl.when(c == n_chunk - 1)
    def _():
        sfin_ref[...] = st_scr[...]


def _hgrn(z, lb, s0t, n_seq, seq_len):
    n_chunk = seq_len // CH
    t = n_seq * seq_len

    def fwd(col):
        return pl.BlockSpec((CH, COL_BLOCK), lambda s, c: (s * n_chunk + c, col))

    def bwd(col):
        return pl.BlockSpec((CH, COL_BLOCK), lambda s, c: (s * n_chunk + n_chunk - 1 - c, col))

    state_spec = pl.BlockSpec((None, 2, H_B, DV, DK), lambda s, c: (s, 0, 0, 0, 0))
    return pl.pallas_call(
        functools.partial(_hgrn_kernel, n_chunk=n_chunk),
        grid=(n_seq, n_chunk),
        in_specs=[fwd(2), fwd(3), fwd(5), bwd(2), bwd(4), bwd(5),
                  pl.BlockSpec((2, QK_WIDTH), lambda s, c: (0, 0)), state_spec],
        out_specs=[fwd(0), bwd(0), state_spec],
        out_shape=[_sds((t, H_B * DV), F32), _sds((t, H_B * DV), F32),
                   _sds((n_seq, 2, H_B, DV, DK), F32)],
        scratch_shapes=[pltpu.VMEM((2, H_B, DV, DK), F32),
                        pltpu.VMEM((2, CH, QK_WIDTH), F32),
                        pltpu.VMEM((2, CH, QK_WIDTH), F32)],
        compiler_params=_params("arbitrary", "arbitrary"),
        name="hgrn",
    )(z, z, z, z, z, z, lb, s0t)


def _merge_kernel(*refs, add_pos):
    refs = list(refs)
    x_ref = refs.pop(0)
    pos_ref = refs.pop(0) if add_pos else None
    (zu_ref, zv_ref, zg_ref, za0_ref, za1_ref, zb0_ref, zb1_ref, ofw_ref, obw_ref, mod_ref,
     sgug_ref, ws_ref, bst_ref, hng_ref, wpa_ref, wpb_ref, wout_ref, n2g_ref,
     x1_ref, hn2_ref) = refs

    u = jax.nn.gelu(zu_ref[...])
    v = (_rms(jax.nn.gelu(zv_ref[...])) * sgug_ref[...]).astype(BF16)
    row_parts = []
    for ci in range(TB_MERGE // CHUNK_MLP):
        col_parts = []
        for g in range(A_GROUPS):
            vg = v[ci * CHUNK_MLP:(ci + 1) * CHUNK_MLP, g * 128:(g + 1) * 128]
            col_parts.append(_dot(ws_ref[g], vg) + bst_ref[:, g:g + 1])
        row_parts.append(jnp.concatenate(col_parts, axis=1))
    y_a = u * jnp.concatenate(row_parts, axis=0)

    o = ofw_ref[...] + obw_ref[...]
    on = jnp.concatenate([_rms(o[:, h * DV:(h + 1) * DV]) for h in range(H_B)], axis=1)
    y_b = on * hng_ref[...] * jax.nn.silu(zg_ref[...])

    pa = _dot(y_a.astype(BF16), wpa_ref[...])
    pb = _dot(y_b.astype(BF16), wpb_ref[...])
    za = jnp.concatenate([za0_ref[...], za1_ref[...]], axis=1)
    zb = jnp.concatenate([zb0_ref[...], zb1_ref[...]], axis=1)
    mix_in = jax.nn.sigmoid(za) * pa + jax.nn.sigmoid(zb) * pb
    mix = _dot(mix_in.astype(BF16), wout_ref[...])

    x = x_ref[...]
    if add_pos:
        x = x + pos_ref[...]
    x1 = x + mod_ref[2:3, :] * mix
    x1_ref[...] = x1
    hn2 = _rms(x1) * n2g_ref[...] * (1.0 + mod_ref[4:5, :]) + mod_ref[3:4, :]
    hn2_ref[...] = hn2.astype(BF16)


def _merge(x2d, pos, z, o_fw, o_bw, mod, mod_row, seq_len, sgu_g, ws_bf, bs_t, hgrn_g,
           wpa_bf, wpb_bf, wout_bf, norm2_g):
    t = x2d.shape[0]
    tb = TB_MERGE
    bps = seq_len // tb
    add_pos = pos is not None

    def zcol(k):
        return pl.BlockSpec((tb, COL_BLOCK), lambda i: (i, k))

    def full(shape):
        return pl.BlockSpec(shape, lambda i: (0,) * len(shape))

    in_specs = [pl.BlockSpec((tb, D_MODEL), lambda i: (i, 0))]
    args = [x2d]
    if add_pos:
        in_specs.append(pl.BlockSpec((tb, D_MODEL), lambda i: (i % bps, 0)))
        args.append(pos)
    in_specs += [zcol(0), zcol(1), zcol(6), zcol(7), zcol(8), zcol(9), zcol(10),
                 pl.BlockSpec((tb, A_WIDTH), lambda i: (i, 0)),
                 pl.BlockSpec((tb, A_WIDTH), lambda i: (i, 0)),
                 pl.BlockSpec((None, N_MOD, D_MODEL), lambda i: (mod_row(i * tb), 0, 0)),
                 full((1, A_WIDTH)), full((A_GROUPS, CHUNK_MLP, CHUNK_MLP)),
                 full((CHUNK_MLP, A_GROUPS)), full((1, H_B * DV)),
                 full((A_WIDTH, D_MODEL)), full((H_B * DV, D_MODEL)),
                 full((D_MODEL, D_MODEL)), full((1, D_MODEL))]
    args += [z] * 7 + [o_fw, o_bw, mod, sgu_g, ws_bf, bs_t, hgrn_g, wpa_bf, wpb_bf, wout_bf, norm2_g]
    return pl.pallas_call(
        functools.partial(_merge_kernel, add_pos=add_pos),
        grid=(t // tb,),
        in_specs=in_specs,
        out_specs=[pl.BlockSpec((tb, D_MODEL), lambda i: (i, 0)),
                   pl.BlockSpec((tb, D_MODEL), lambda i: (i, 0))],
        out_shape=[_sds((t, D_MODEL), F32), _sds((t, D_MODEL), BF16)],
        compiler_params=_params("arbitrary"),
        name="merge",
    )(*args)


def _sort_network(n):
    def merge(lo, hi, r):
        step = r * 2
        if step < hi - lo:
            yield from merge(lo, hi, step)
            yield from merge(lo + r, hi, step)
            yield from [(i, i + r) for i in range(lo + r, hi - r, step)]
        else:
            yield (lo, lo + r)

    def sort(lo, hi):
        if hi - lo >= 1:
            mid = lo + (hi - lo) // 2
            yield from sort(lo, mid)
            yield from sort(mid + 1, hi)
            yield from merge(lo, hi, 1)

    return tuple(sort(0, n - 1))


def _col_max(x):
    return jnp.max(x, axis=0, keepdims=True)


def _col_min(x):
    return jnp.min(x, axis=0, keepdims=True)


def _topk_kernel(hn_ref, wq_ref, keys_ref, idx_ref, g_ref, v_scr, i_scr, tv_scr, fp_scr):
    tb = hn_ref.shape[0]
    k = PEER_TOPK
    q = _dot(hn_ref[...], wq_ref[...]).astype(BF16)
    sub = lax.broadcasted_iota(I32, (SUBLANES, tb), 0).astype(F32)
    n_tiles = N_KEYS // SUBLANES
    neg = -jnp.inf

    for h in range(PEER_HEADS):
        for p in range(2):
            hp = 2 * h + p
            s = _dot_nt(keys_ref[hp], q[:, hp * 128:(hp + 1) * 128])
            vals = [s[SUBLANES * v:SUBLANES * (v + 1), :] for v in range(n_tiles)]
            ids = [sub + float(SUBLANES * v) for v in range(n_tiles)]
            for i, j in _sort_network(n_tiles):
                a, b, ia, ib = vals[i], vals[j], ids[i], ids[j]
                swap = (b > a) | ((b == a) & (ib < ia))
                vals[i], vals[j] = jnp.where(swap, b, a), jnp.where(swap, a, b)
                ids[i], ids[j] = jnp.where(swap, ib, ia), jnp.where(swap, ia, ib)
            for r in range(k):
                m = _col_max(vals[0])
                first = _col_min(jnp.where(vals[0] == m, ids[0], float(N_KEYS)))
                sel = ids[0] == first
                v_scr[p, r:r + 1, :] = m
                i_scr[p, r:r + 1, :] = first.astype(I32)
                left = k - 1 - r
                for d in range(left):
                    vals[d] = jnp.where(sel, vals[d + 1], vals[d])
                    ids[d] = jnp.where(sel, ids[d + 1], ids[d])

        lists = []
        for r in range(k):
            val = v_scr[0, r:r + 1, :] + v_scr[1, 0:SUBLANES, :]
            n_valid = k // (r + 1)
            lists.append(val if n_valid >= SUBLANES else jnp.where(sub < float(n_valid), val, neg))
        extra = v_scr[0, 0:1, :] + v_scr[1, SUBLANES:k, :]
        flat = sub
        flat_extra = sub + float(SUBLANES)
        for j in range(k):
            m = _col_max(jnp.maximum(lists[0], extra))
            first = _col_min(jnp.minimum(jnp.where(lists[0] == m, flat, float(k * k)),
                                         jnp.where(extra == m, flat_extra, float(k * k))))
            tv_scr[j:j + 1, :] = m
            fp_scr[j:j + 1, :] = first.astype(I32)
            left = k - 1 - j
            if left:
                sel = flat == first
                for d in range(left):
                    lists[d] = jnp.where(sel, lists[d + 1], lists[d])
                flat = jnp.where(sel, flat + float(k), flat)
                extra = jnp.where(flat_extra == first, neg, extra)

        fp = fp_scr[...]
        pr = fp >> 4
        pc = fp & (k - 1)
        e1 = jnp.zeros_like(fp)
        e2 = jnp.zeros_like(fp)
        for r in range(k):
            e1 = jnp.where(pr == r, i_scr[0, r:r + 1, :], e1)
            e2 = jnp.where(pc == r, i_scr[1, r:r + 1, :], e2)
        idx_ref[h * k:(h + 1) * k, :] = e1 * N_KEYS + e2
        tv = tv_scr[...]
        ex = jnp.exp(tv - tv[0:1, :])
        g_ref[h * k:(h + 1) * k, :] = ex / jnp.sum(ex, axis=0, keepdims=True)


def _topk(hn2, wq_bf, keys_bf):
    t = hn2.shape[0]
    tb = TB_TOPK
    hk = PEER_HEADS * PEER_TOPK
    return pl.pallas_call(
        _topk_kernel,
        grid=(t // tb,),
        in_specs=[pl.BlockSpec((tb, D_MODEL), lambda i: (i, 0)),
                  pl.BlockSpec((D_MODEL, PEER_Q), lambda i: (0, 0)),
                  pl.BlockSpec((2 * PEER_HEADS, N_KEYS, 128), lambda i: (0, 0, 0))],
        out_specs=[pl.BlockSpec((hk, tb), lambda i: (0, i)),
                   pl.BlockSpec((hk, tb), lambda i: (0, i))],
        out_shape=[_sds((hk, t), I32), _sds((hk, t), F32)],
        scratch_shapes=[pltpu.VMEM((2, PEER_TOPK, tb), F32),
                        pltpu.VMEM((2, PEER_TOPK, tb), I32),
                        pltpu.VMEM((PEER_TOPK, tb), F32),
                        pltpu.VMEM((PEER_TOPK, tb), I32)],
        compiler_params=_params("arbitrary"),
        name="peer_topk",
    )(hn2, wq_bf, keys_bf)


def _stream_copy(src_hbm, buf, sem, step, block):
    slot = lax.rem(step, STREAM_SLOTS)
    return pltpu.make_async_copy(src_hbm.at[pl.ds(block * E_BLOCK, E_BLOCK)], buf.at[slot], sem.at[slot])


def _stream_prefetch(copy_for_step, step, total):
    @pl.when(step == 0)
    def _():
        for k in range(STREAM_AHEAD):
            copy_for_step(k).start()

    @pl.when(step + STREAM_AHEAD < total)
    def _():
        copy_for_step(step + STREAM_AHEAD).start()


def _peer_up_kernel(hn_ref, u_hbm, idx_ref, p_ref, s_even, s_odd, ubuf, usem):
    j = pl.program_id(1)
    nj = pl.num_programs(1)
    step_id = pl.program_id(0) * nj + j
    per = E_BLOCK // N_KEYS

    def u_copy(s):
        return _stream_copy(u_hbm, ubuf, usem, s, jnp.minimum(lax.rem(s, nj), nj - 2))

    _stream_prefetch(u_copy, step_id, pl.num_programs(0) * nj)

    @pl.when(j == 0)
    def _():
        p_ref[...] = jnp.zeros_like(p_ref)
        s_odd[...] = jnp.zeros_like(s_odd)

    u_copy(step_id).wait()
    slot = lax.rem(step_id, STREAM_SLOTS)

    def step(cur, prev):
        cur[...] = _dot_nt(hn_ref[...], ubuf[slot].astype(BF16))
        first_row = (j - 1) * per
        for c in range(TB_UP // SUBLANES):
            rows = slice(c * SUBLANES, (c + 1) * SUBLANES)
            idx = idx_ref[rows, :]
            i1 = idx >> 7
            i2 = idx & (N_KEYS - 1)
            acc = p_ref[rows, :]
            for k in range(per):
                picked = jnp.take_along_axis(prev[rows, k * N_KEYS:(k + 1) * N_KEYS], i2, axis=1)
                acc = jnp.where(i1 == first_row + k, picked, acc)
            p_ref[rows, :] = acc

    @pl.when(j % 2 == 0)
    def _():
        step(s_even, s_odd)

    @pl.when(j % 2 == 1)
    def _():
        step(s_odd, s_even)


def _peer_up(hn2, u, idx):
    t = hn2.shape[0]
    hk = PEER_HEADS * PEER_TOPK
    n_blocks = N_EXPERTS // E_BLOCK
    return pl.pallas_call(
        _peer_up_kernel,
        grid=(t // TB_UP, n_blocks + 1),
        in_specs=[pl.BlockSpec((TB_UP, D_MODEL), lambda i, j: (i, 0)),
                  pl.BlockSpec(memory_space=pl.ANY),
                  pl.BlockSpec((TB_UP, hk), lambda i, j: (i, 0))],
        out_specs=pl.BlockSpec((TB_UP, hk), lambda i, j: (i, 0)),
        out_shape=_sds((t, hk), F32),
        scratch_shapes=[pltpu.VMEM((TB_UP, E_BLOCK), F32), pltpu.VMEM((TB_UP, E_BLOCK), F32),
                        pltpu.VMEM((STREAM_SLOTS, E_BLOCK, D_MODEL), u.dtype),
                        pltpu.SemaphoreType.DMA((STREAM_SLOTS,))],
        compiler_params=_params("arbitrary", "arbitrary"),
        name="peer_up",
    )(hn2, u, idx)


def _peer_down_kernel(p_ref, g_ref, idx_ref, v_hbm, x1_ref, mod_ref, fg_ref, y_ref, a_scr, abuf, vbuf, vsem):
    j = pl.program_id(1)
    nj = pl.num_programs(1)
    step_id = pl.program_id(0) * nj + j

    def v_copy(s):
        return _stream_copy(v_hbm, vbuf, vsem, s, lax.rem(s, nj))

    _stream_prefetch(v_copy, step_id, pl.num_programs(0) * nj)
    v_copy(step_id).wait()

    @pl.when(j == 0)
    def _():
        y_ref[...] = jnp.zeros_like(y_ref)
        a_scr[...] = jax.nn.gelu(p_ref[...]) * g_ref[...]
        key_ids = lax.broadcasted_iota(I32, (N_KEYS, PEER_HEADS * PEER_TOPK), 0)

        def body(tt, carry):
            for u in range(SCATTER_UNROLL):
                t = tt * SCATTER_UNROLL + u
                idx = idx_ref[pl.ds(t, 1), :]
                pm = jnp.where((idx >> 7) == key_ids, a_scr[pl.ds(t, 1), :], 0.0).astype(BF16)
                qm = jnp.where((idx & (N_KEYS - 1)) == key_ids, 1.0, 0.0).astype(BF16)
                abuf[pl.ds(pl.multiple_of(t * A_STRIDE, SUBLANES), N_KEYS), :] = _dot_nt(pm, qm)
            return carry

        lax.fori_loop(0, TB_DOWN // SCATTER_UNROLL, body, 0)

    per = E_BLOCK // N_KEYS
    cols = [abuf[pl.ds(j * per + k, TB_DOWN, stride=A_STRIDE), :] for k in range(per)]
    a_blk = jnp.concatenate(cols, axis=1).astype(BF16)
    y_ref[...] += _dot(a_blk, vbuf[lax.rem(step_id, STREAM_SLOTS)])

    @pl.when(j == nj - 1)
    def _():
        x2 = x1_ref[...] + mod_ref[5:6, :] * y_ref[...]
        y_ref[...] = _rms(x2) * fg_ref[...]


def _peer_down(p, g, idx, v_bf, x1, mod, mod_row, final_g):
    t = p.shape[0]
    hk = PEER_HEADS * PEER_TOPK
    tok = pl.BlockSpec((TB_DOWN, hk), lambda i, j: (i, 0))
    wide = pl.BlockSpec((TB_DOWN, D_MODEL), lambda i, j: (i, 0))
    return pl.pallas_call(
        _peer_down_kernel,
        grid=(t // TB_DOWN, N_EXPERTS // E_BLOCK),
        in_specs=[tok, tok, tok, pl.BlockSpec(memory_space=pl.ANY), wide,
                  pl.BlockSpec((None, N_MOD, D_MODEL), lambda i, j: (mod_row(i * TB_DOWN), 0, 0)),
                  pl.BlockSpec((1, D_MODEL), lambda i, j: (0, 0))],
        out_specs=wide,
        out_shape=_sds((t, D_MODEL), F32),
        scratch_shapes=[pltpu.VMEM((TB_DOWN, hk), F32),
                        pltpu.VMEM((TB_DOWN * A_STRIDE, N_KEYS), F32),
                        pltpu.VMEM((STREAM_SLOTS, E_BLOCK, D_MODEL), v_bf.dtype),
                        pltpu.SemaphoreType.DMA((STREAM_SLOTS,))],
        compiler_params=_params("arbitrary", "arbitrary"),
        name="peer_down",
    )(p, g, idx, v_bf, x1, mod, final_g)


def _grid_pos_embed(n_tokens):
    rows = n_tokens // GRID_W
    r = np.repeat(np.arange(rows, dtype=np.float64), GRID_W)
    col = np.tile(np.arange(GRID_W, dtype=np.float64), rows)
    quarter = D_MODEL // 4
    omega = 1.0 / (POS_THETA ** (np.arange(quarter, dtype=np.float64) / quarter))
    ar = r[:, None] * omega
    ac = col[:, None] * omega
    table = np.concatenate([np.sin(ar), np.cos(ar), np.sin(ac), np.cos(ac)], axis=-1)
    return jnp.asarray(table.astype(np.float32))


def _trunk(x, pos, mod, mod_base, per_seq, s0t, lb, w):
    n_seq, seq_len, _ = x.shape
    x2d = x.reshape(n_seq * seq_len, D_MODEL)

    def mod_row(tok):
        return mod_base + (tok // seq_len if per_seq else 0)

    z = _inproj(x2d, pos, mod, mod_row, w["norm1_g"], w["w_in"], seq_len)
    o_fw, o_bw, st_fin = _hgrn(z, lb, s0t, n_seq, seq_len)
    x1, hn2 = _merge(x2d, pos, z, o_fw, o_bw, mod, mod_row, seq_len, w["sgu_norm_g"], w["w_spatial"],
                     w["b_spatial_t"], w["hgrn_norm_g"], w["w_proj_a"], w["w_proj_b"], w["w_out"],
                     w["norm2_g"])
    idx_t, g_t = _topk(hn2, w["peer_w_q"], w["peer_sub_keys"])
    idx = idx_t.T
    p = _peer_up(hn2, w["peer_u"], idx)
    y = _peer_down(p, g_t.T, idx, w["peer_v"], x1, mod, mod_row, w["final_norm_g"])
    return y.reshape(x.shape), jnp.swapaxes(st_fin, -1, -2)


def kernel(x_prompt, x_sample, state_hgrn, c, c_ctx, w_ada, b_ada, norm1_g, w_in, sgu_norm_g,
           w_spatial, b_spatial, hgrn_lb, hgrn_norm_g, w_proj_a, w_proj_b, w_out, norm2_g,
           peer_w_q, peer_sub_keys, peer_u, peer_v, final_norm_g):
    assert w_in.shape[0] == 1, "single-layer stack"
    n_p, l_p, _ = x_prompt.shape
    n_s, l_s, _ = x_sample.shape
    assert n_s + 1 <= SUBLANES
    lb = jnp.cumsum(jax.nn.softmax(hgrn_lb.astype(F32), axis=0), axis=0)[0]

    cond = jnp.zeros((SUBLANES, D_MODEL), F32).at[0].set(c_ctx).at[1:1 + n_s].set(c)
    mod = _adaln(cond, w_ada[0], b_ada[0][None]).reshape(SUBLANES, N_MOD, D_MODEL)

    w = dict(
        norm1_g=norm1_g[0][None], w_in=w_in[0].astype(BF16), sgu_norm_g=sgu_norm_g[0][None],
        w_spatial=w_spatial[0].astype(BF16), b_spatial_t=b_spatial[0].T,
        hgrn_norm_g=hgrn_norm_g[0].reshape(1, H_B * DV),
        w_proj_a=w_proj_a[0].astype(BF16), w_proj_b=w_proj_b[0].astype(BF16),
        w_out=w_out[0].astype(BF16), norm2_g=norm2_g[0][None],
        peer_w_q=peer_w_q[0].astype(BF16),
        peer_sub_keys=peer_sub_keys[0].reshape(2 * PEER_HEADS, N_KEYS, 128).astype(BF16),
        peer_u=peer_u[0], peer_v=peer_v[0].astype(BF16),
        final_norm_g=final_norm_g[None],
    )

    s0_ctx = jnp.zeros((n_p, 2, H_B, DV, DK), F32)
    s0_lat = jnp.swapaxes(state_hgrn[:, 0].astype(F32), -1, -2)
    pos = _grid_pos_embed(l_s)

    y_prompt, s_ctx = _trunk(x_prompt, None, mod, 0, False, s0_ctx, lb, w)
    y_sample, _ = _trunk(x_sample, pos, mod, 1, True, s0_lat, lb, w)
    return (y_prompt, y_sample, s_ctx[:, None])
```

```python
import functools

import jax
import jax.numpy as jnp
import numpy as np
from jax import lax
from jax.experimental import pallas as pl
from jax.experimental.pallas import tpu as pltpu

F32 = jnp.float32
BF16 = jnp.bfloat16
I32 = jnp.int32

D_MODEL = 1024
N_MOD = 6
A_WIDTH = 512
A_GROUPS = 4
CHUNK_MLP = 128
H_B = 4
DK = 128
DV = 128
QK_WIDTH = H_B * DK
IN_WIDTH = 5632
COL_BLOCK = 512
PEER_HEADS = 8
N_KEYS = 128
PEER_TOPK = 16
PEER_Q = 2 * PEER_HEADS * 128
N_EXPERTS = N_KEYS * N_KEYS
GRID_W = 64
POS_THETA = 10000.0
EPS = 1e-6

SUBLANES = 8
LANES = 128
VMEM_LIMIT_BYTES = 56 * 1024 * 1024

TB_IN = 512
CH = 128
TB_MERGE = 256
TB_TOPK = 256
TB_UP = 1024
TB_DOWN = 512
E_BLOCK = 1024
A_STRIDE = 136
SCATTER_UNROLL = 32
STREAM_AHEAD = 2
STREAM_SLOTS = STREAM_AHEAD + 1


def _sds(shape, dtype):
    return jax.ShapeDtypeStruct(shape, dtype)


def _params(*sem):
    return pltpu.CompilerParams(dimension_semantics=sem, vmem_limit_bytes=VMEM_LIMIT_BYTES)


def _dot(a, b):
    return jnp.dot(a, b, preferred_element_type=F32)


def _dot_nt(a, b):
    return lax.dot_general(a, b, (((1,), (1,)), ((), ())), preferred_element_type=F32)


def _dot_tn(a, b):
    return lax.dot_general(a, b, (((0,), (0,)), ((), ())), preferred_element_type=F32)


def _rms(x):
    return x * lax.rsqrt(jnp.mean(x * x, axis=-1, keepdims=True) + EPS)


def _adaln_kernel(c_ref, w_ref, b_ref, o_ref):
    a = jax.nn.silu(c_ref[...])
    o_ref[...] = jnp.dot(a, w_ref[...], preferred_element_type=F32,
                         precision=lax.Precision.HIGHEST) + b_ref[...]


def _adaln(cond8, w_ada, b_ada):
    n = w_ada.shape[1]
    tn = 1536
    return pl.pallas_call(
        _adaln_kernel,
        grid=(n // tn,),
        in_specs=[pl.BlockSpec((SUBLANES, D_MODEL), lambda j: (0, 0)),
                  pl.BlockSpec((D_MODEL, tn), lambda j: (0, j)),
                  pl.BlockSpec((1, tn), lambda j: (0, j))],
        out_specs=pl.BlockSpec((SUBLANES, tn), lambda j: (0, j)),
        out_shape=_sds((SUBLANES, n), F32),
        compiler_params=_params("arbitrary"),
        name="adaln",
    )(cond8, w_ada, b_ada)


def _inproj_kernel(*refs, add_pos):
    if add_pos:
        x_ref, pos_ref, mod_ref, g_ref, w_ref, z_ref = refs
    else:
        x_ref, mod_ref, g_ref, w_ref, z_ref = refs
    x = x_ref[...]
    if add_pos:
        x = x + pos_ref[...]
    h = _rms(x) * g_ref[...] * (1.0 + mod_ref[1:2, :]) + mod_ref[0:1, :]
    z_ref[...] = _dot(h.astype(BF16), w_ref[...])


def _inproj(x2d, pos, mod, mod_row, norm_g, w_in_bf, seq_len):
    t = x2d.shape[0]
    bps = seq_len // TB_IN
    add_pos = pos is not None
    in_specs = [pl.BlockSpec((TB_IN, D_MODEL), lambda i: (i, 0))]
    args = [x2d]
    if add_pos:
        in_specs.append(pl.BlockSpec((TB_IN, D_MODEL), lambda i: (i % bps, 0)))
        args.append(pos)
    in_specs += [pl.BlockSpec((None, N_MOD, D_MODEL), lambda i: (mod_row(i * TB_IN), 0, 0)),
                 pl.BlockSpec((1, D_MODEL), lambda i: (0, 0)),
                 pl.BlockSpec((D_MODEL, IN_WIDTH), lambda i: (0, 0), pipeline_mode=pl.Buffered(1))]
    args += [mod, norm_g, w_in_bf]
    return pl.pallas_call(
        functools.partial(_inproj_kernel, add_pos=add_pos),
        grid=(t // TB_IN,),
        in_specs=in_specs,
        out_specs=pl.BlockSpec((TB_IN, IN_WIDTH), lambda i: (i, 0)),
        out_shape=_sds((t, IN_WIDTH), F32),
        compiler_params=_params("arbitrary"),
        name="inproj",
    )(*args)


def _split3(x):
    hi = x.astype(BF16)
    r1 = x - hi.astype(F32)
    mid = r1.astype(BF16)
    lo = (r1 - mid.astype(F32)).astype(BF16)
    return hi, mid, lo


def _hgrn_direction(zq_ref, zf_ref, zv_ref, lb, st_scr, b_scr, k_scr, o_ref, reverse):
    f = lb + (1.0 - lb) * jax.nn.sigmoid(zf_ref[...])
    k_scr[...] = 1.0 - f
    lf = jnp.log(f)
    row = lax.broadcasted_iota(I32, (CH, CH), 0)
    col = lax.broadcasted_iota(I32, (CH, CH), 1)
    tri = jnp.where((row <= col) if reverse else (row >= col), 1.0, 0.0).astype(BF16)
    hi, mid, lo = _split3(lf)
    b_scr[...] = _dot(tri, hi) + _dot(tri, mid) + _dot(tri, lo)

    sub = lax.broadcasted_iota(I32, (SUBLANES, DK), 0)
    zero_tile = jnp.zeros((SUBLANES, DK), F32)
    n_tiles = CH // SUBLANES
    end_row = 0 if reverse else CH - 1

    for h in range(H_B):
        sl = slice(h * DK, (h + 1) * DK)
        q = zq_ref[:, sl]
        kk = k_scr[:, sl]
        bh = b_scr[:, sl]

        def masked(sc, m):
            if 2 * m >= CH:
                return sc
            shift = (2 * m).bit_length() - 1 if m else 0
            return jnp.where((row >> shift) == (col >> shift), sc, 0.0)

        s_tot = masked(_dot_nt(q.astype(BF16), kk.astype(BF16)), 0)

        m = 1
        while m < CH:
            q_tiles, k_tiles = [], []
            for j in range(n_tiles):
                r0 = j * SUBLANES
                rows = slice(r0, r0 + SUBLANES)
                if m < SUBLANES:
                    upper = (sub & m) != 0
                    gives_q = ~upper if reverse else upper
                    if m == 1:
                        scale_q = 1.0 - k_scr[rows, sl]
                        scale_k = None
                    else:
                        seam = m if reverse else m - 1
                        if 2 * m == SUBLANES:
                            b_mid = b_scr[r0 + seam:r0 + seam + 1, sl]
                        else:
                            b_mid = jnp.where(sub < 2 * m, b_scr[r0 + seam:r0 + seam + 1, sl],
                                              b_scr[r0 + 2 * m + seam:r0 + 2 * m + seam + 1, sl])
                        d = b_scr[rows, sl] - b_mid
                        scale_q = scale_k = jnp.exp(jnp.minimum(d, -d))
                    qt = zq_ref[rows, sl] * scale_q
                    kt = k_scr[rows, sl] if scale_k is None else k_scr[rows, sl] * scale_k
                    q_tiles.append(jnp.where(gives_q, qt, 0.0))
                    k_tiles.append(jnp.where(gives_q, 0.0, kt))
                else:
                    blk = r0 // m
                    gives_q = (blk % 2 == 0) if reverse else (blk % 2 == 1)
                    if gives_q:
                        ref_row = blk * m + m if reverse else blk * m - 1
                        q_tiles.append(zq_ref[rows, sl] * jnp.exp(
                            b_scr[rows, sl] - b_scr[ref_row:ref_row + 1, sl]))
                        k_tiles.append(zero_tile)
                    else:
                        ref_row = blk * m if reverse else blk * m + m - 1
                        k_tiles.append(k_scr[rows, sl] * jnp.exp(
                            b_scr[ref_row:ref_row + 1, sl] - b_scr[rows, sl]))
                        q_tiles.append(zero_tile)
            qm = jnp.concatenate(q_tiles, axis=0).astype(BF16)
            km = jnp.concatenate(k_tiles, axis=0).astype(BF16)
            s_tot = s_tot + masked(_dot_nt(qm, km), m)
            m *= 2

        v = zv_ref[:, sl].astype(BF16)
        st = st_scr[h]
        o_ref[:, sl] = (_dot(s_tot.astype(BF16), v)
                        + _dot_nt((q * jnp.exp(bh)).astype(BF16), st.astype(BF16)))
        b_end = b_scr[end_row:end_row + 1, sl]
        ke = (kk * jnp.exp(b_end - bh)).astype(BF16)
        st_scr[h] = st * jnp.exp(b_end) + _dot_tn(v, ke)


def _hgrn_kernel(zqf_ref, zff_ref, zvf_ref, zqb_ref, zfb_ref, zvb_ref, lb_ref, s0_ref,
                 ofw_ref, obw_ref, sfin_ref, st_scr, b_scr, k_scr, *, n_chunk):
    c = pl.program_id(1)

    @pl.when(c == 0)
    def _():
        st_scr[...] = s0_ref[...]

    _hgrn_direction(zqf_ref, zff_ref, zvf_ref, lb_ref[0:1, :], st_scr.at[0], b_scr.at[0],
                    k_scr.at[0], ofw_ref, reverse=False)
    _hgrn_direction(zqb_ref, zfb_ref, zvb_ref, lb_ref[1:2, :], st_scr.at[1], b_scr.at[1],
                    k_scr.at[1], obw_ref, reverse=True)

    @pl.when(c == n_chunk - 1)
    def _():
        sfin_ref[...] = st_scr[...]


def _hgrn(z, lb, s0t, n_seq, seq_len):
    n_chunk = seq_len // CH
    t = n_seq * seq_len

    def fwd(col):
        return pl.BlockSpec((CH, COL_BLOCK), lambda s, c: (s * n_chunk + c, col))

    def bwd(col):
        return pl.BlockSpec((CH, COL_BLOCK), lambda s, c: (s * n_chunk + n_chunk - 1 - c, col))

    state_spec = pl.BlockSpec((None, 2, H_B, DV, DK), lambda s, c: (s, 0, 0, 0, 0))
    return pl.pallas_call(
        functools.partial(_hgrn_kernel, n_chunk=n_chunk),
        grid=(n_seq, n_chunk),
        in_specs=[fwd(2), fwd(3), fwd(5), bwd(2), bwd(4), bwd(5),
                  pl.BlockSpec((2, QK_WIDTH), lambda s, c: (0, 0)), state_spec],
        out_specs=[fwd(0), bwd(0), state_spec],
        out_shape=[_sds((t, H_B * DV), F32), _sds((t, H_B * DV), F32),
                   _sds((n_seq, 2, H_B, DV, DK), F32)],
        scratch_shapes=[pltpu.VMEM((2, H_B, DV, DK), F32),
                        pltpu.VMEM((2, CH, QK_WIDTH), F32),
                        pltpu.VMEM((2, CH, QK_WIDTH), F32)],
        compiler_params=_params("arbitrary", "arbitrary"),
        name="hgrn",
    )(z, z, z, z, z, z, lb, s0t)


def _merge_kernel(*refs, add_pos):
    refs = list(refs)
    x_ref = refs.pop(0)
    pos_ref = refs.pop(0) if add_pos else None
    (zu_ref, zv_ref, zg_ref, za0_ref, za1_ref, zb0_ref, zb1_ref, ofw_ref, obw_ref, mod_ref,
     sgug_ref, ws_ref, bst_ref, hng_ref, wpa_ref, wpb_ref, wout_ref, n2g_ref,
     x1_ref, hn2_ref) = refs

    u = jax.nn.gelu(zu_ref[...])
    v = (_rms(jax.nn.gelu(zv_ref[...])) * sgug_ref[...]).astype(BF16)
    row_parts = []
    for ci in range(TB_MERGE // CHUNK_MLP):
        col_parts = []
        for g in range(A_GROUPS):
            vg = v[ci * CHUNK_MLP:(ci + 1) * CHUNK_MLP, g * 128:(g + 1) * 128]
            col_parts.append(_dot(ws_ref[g], vg) + bst_ref[:, g:g + 1])
        row_parts.append(jnp.concatenate(col_parts, axis=1))
    y_a = u * jnp.concatenate(row_parts, axis=0)

    o = ofw_ref[...] + obw_ref[...]
    on = jnp.concatenate([_rms(o[:, h * DV:(h + 1) * DV]) for h in range(H_B)], axis=1)
    y_b = on * hng_ref[...] * jax.nn.silu(zg_ref[...])

    pa = _dot(y_a.astype(BF16), wpa_ref[...])
    pb = _dot(y_b.astype(BF16), wpb_ref[...])
    za = jnp.concatenate([za0_ref[...], za1_ref[...]], axis=1)
    zb = jnp.concatenate([zb0_ref[...], zb1_ref[...]], axis=1)
    mix_in = jax.nn.sigmoid(za) * pa + jax.nn.sigmoid(zb) * pb
    mix = _dot(mix_in.astype(BF16), wout_ref[...])

    x = x_ref[...]
    if add_pos:
        x = x + pos_ref[...]
    x1 = x + mod_ref[2:3, :] * mix
    x1_ref[...] = x1
    hn2 = _rms(x1) * n2g_ref[...] * (1.0 + mod_ref[4:5, :]) + mod_ref[3:4, :]
    hn2_ref[...] = hn2.astype(BF16)


def _merge(x2d, pos, z, o_fw, o_bw, mod, mod_row, seq_len, sgu_g, ws_bf, bs_t, hgrn_g,
           wpa_bf, wpb_bf, wout_bf, norm2_g):
    t = x2d.shape[0]
    tb = TB_MERGE
    bps = seq_len // tb
    add_pos = pos is not None

    def zcol(k):
        return pl.BlockSpec((tb, COL_BLOCK), lambda i: (i, k))

    def full(shape):
        return pl.BlockSpec(shape, lambda i: (0,) * len(shape))

    in_specs = [pl.BlockSpec((tb, D_MODEL), lambda i: (i, 0))]
    args = [x2d]
    if add_pos:
        in_specs.append(pl.BlockSpec((tb, D_MODEL), lambda i: (i % bps, 0)))
        args.append(pos)
    in_specs += [zcol(0), zcol(1), zcol(6), zcol(7), zcol(8), zcol(9), zcol(10),
                 pl.BlockSpec((tb, A_WIDTH), lambda i: (i, 0)),
                 pl.BlockSpec((tb, A_WIDTH), lambda i: (i, 0)),
                 pl.BlockSpec((None, N_MOD, D_MODEL), lambda i: (mod_row(i * tb), 0, 0)),
                 full((1, A_WIDTH)), full((A_GROUPS, CHUNK_MLP, CHUNK_MLP)),
                 full((CHUNK_MLP, A_GROUPS)), full((1, H_B * DV)),
                 full((A_WIDTH, D_MODEL)), full((H_B * DV, D_MODEL)),
                 full((D_MODEL, D_MODEL)), full((1, D_MODEL))]
    args += [z] * 7 + [o_fw, o_bw, mod, sgu_g, ws_bf, bs_t, hgrn_g, wpa_bf, wpb_bf, wout_bf, norm2_g]
    return pl.pallas_call(
        functools.partial(_merge_kernel, add_pos=add_pos),
        grid=(t // tb,),
        in_specs=in_specs,
        out_specs=[pl.BlockSpec((tb, D_MODEL), lambda i: (i, 0)),
                   pl.BlockSpec((tb, D_MODEL), lambda i: (i, 0))],
        out_shape=[_sds((t, D_MODEL), F32), _sds((t, D_MODEL), BF16)],
        compiler_params=_params("arbitrary"),
        name="merge",
    )(*args)


def _sort_network(n):
    def merge(lo, hi, r):
        step = r * 2
        if step < hi - lo:
            yield from merge(lo, hi, step)
            yield from merge(lo + r, hi, step)
            yield from [(i, i + r) for i in range(lo + r, hi - r, step)]
        else:
            yield (lo, lo + r)

    def sort(lo, hi):
        if hi - lo >= 1:
            mid = lo + (hi - lo) // 2
            yield from sort(lo, mid)
            yield from sort(mid + 1, hi)
            yield from merge(lo, hi, 1)

    return tuple(sort(0, n - 1))


def _col_max(x):
    return jnp.max(x, axis=0, keepdims=True)


def _col_min(x):
    return jnp.min(x, axis=0, keepdims=True)


def _topk_kernel(*refs, cast_table):
    if cast_table:
        (hn_ref, wq_ref, keys_ref, tab_ref, idx_ref, g_ref, tab_out_ref,
         v_scr, i_scr, tv_scr, fp_scr, idxt_scr, gt_scr) = refs
        tab_out_ref[...] = tab_ref[...].astype(BF16)
    else:
        (hn_ref, wq_ref, keys_ref, idx_ref, g_ref,
         v_scr, i_scr, tv_scr, fp_scr, idxt_scr, gt_scr) = refs
    tb = hn_ref.shape[0]
    k = PEER_TOPK
    q = _dot(hn_ref[...], wq_ref[...]).astype(BF16)
    sub = lax.broadcasted_iota(I32, (SUBLANES, tb), 0).astype(F32)
    n_tiles = N_KEYS // SUBLANES
    neg = -jnp.inf

    for h in range(PEER_HEADS):
        for p in range(2):
            hp = 2 * h + p
            s = _dot_nt(keys_ref[hp], q[:, hp * 128:(hp + 1) * 128])
            vals = [s[SUBLANES * v:SUBLANES * (v + 1), :] for v in range(n_tiles)]
            ids = [sub + float(SUBLANES * v) for v in range(n_tiles)]
            for i, j in _sort_network(n_tiles):
                a, b, ia, ib = vals[i], vals[j], ids[i], ids[j]
                swap = (b > a) | ((b == a) & (ib < ia))
                vals[i], vals[j] = jnp.where(swap, b, a), jnp.where(swap, a, b)
                ids[i], ids[j] = jnp.where(swap, ib, ia), jnp.where(swap, ia, ib)
            for r in range(k):
                m = _col_max(vals[0])
                first = _col_min(jnp.where(vals[0] == m, ids[0], float(N_KEYS)))
                sel = ids[0] == first
                v_scr[p, r:r + 1, :] = m
                i_scr[p, r:r + 1, :] = first.astype(I32)
                left = k - 1 - r
                for d in range(left):
                    vals[d] = jnp.where(sel, vals[d + 1], vals[d])
                    ids[d] = jnp.where(sel, ids[d + 1], ids[d])

        lists = []
        for r in range(k):
            val = v_scr[0, r:r + 1, :] + v_scr[1, 0:SUBLANES, :]
            n_valid = k // (r + 1)
            lists.append(val if n_valid >= SUBLANES else jnp.where(sub < float(n_valid), val, neg))
        extra = v_scr[0, 0:1, :] + v_scr[1, SUBLANES:k, :]
        flat = sub
        flat_extra = sub + float(SUBLANES)
        for j in range(k):
            m = _col_max(jnp.maximum(lists[0], extra))
            first = _col_min(jnp.minimum(jnp.where(lists[0] == m, flat, float(k * k)),
                                         jnp.where(extra == m, flat_extra, float(k * k))))
            tv_scr[j:j + 1, :] = m
            fp_scr[j:j + 1, :] = first.astype(I32)
            left = k - 1 - j
            if left:
                sel = flat == first
                for d in range(left):
                    lists[d] = jnp.where(sel, lists[d + 1], lists[d])
                flat = jnp.where(sel, flat + float(k), flat)
                extra = jnp.where(flat_extra == first, neg, extra)

        fp = fp_scr[...]
        pr = fp >> 4
        pc = fp & (k - 1)
        e1 = jnp.zeros_like(fp)
        e2 = jnp.zeros_like(fp)
        for r in range(k):
            e1 = jnp.where(pr == r, i_scr[0, r:r + 1, :], e1)
            e2 = jnp.where(pc == r, i_scr[1, r:r + 1, :], e2)
        idxt_scr[h * k:(h + 1) * k, :] = e1 * N_KEYS + e2
        tv = tv_scr[...]
        ex = jnp.exp(tv - tv[0:1, :])
        gt_scr[h * k:(h + 1) * k, :] = ex / jnp.sum(ex, axis=0, keepdims=True)

    idx_ref[...] = idxt_scr[...].T
    g_ref[...] = gt_scr[...].T


def _topk(hn2, wq_bf, keys_bf, table=None):
    t = hn2.shape[0]
    tb = TB_TOPK
    hk = PEER_HEADS * PEER_TOPK
    steps = t // tb
    in_specs = [pl.BlockSpec((tb, D_MODEL), lambda i: (i, 0)),
                pl.BlockSpec((D_MODEL, PEER_Q), lambda i: (0, 0)),
                pl.BlockSpec((2 * PEER_HEADS, N_KEYS, 128), lambda i: (0, 0, 0))]
    out_specs = [pl.BlockSpec((tb, hk), lambda i: (i, 0)), pl.BlockSpec((tb, hk), lambda i: (i, 0))]
    out_shape = [_sds((t, hk), I32), _sds((t, hk), F32)]
    args = [hn2, wq_bf, keys_bf]
    if table is not None:
        rows = table.shape[0] // steps
        assert rows * steps == table.shape[0] and rows % 16 == 0
        in_specs.append(pl.BlockSpec((rows, table.shape[1]), lambda i: (i, 0)))
        out_specs.append(pl.BlockSpec((rows, table.shape[1]), lambda i: (i, 0)))
        out_shape.append(_sds(table.shape, BF16))
        args.append(table)
    return pl.pallas_call(
        functools.partial(_topk_kernel, cast_table=table is not None),
        grid=(steps,),
        in_specs=in_specs,
        out_specs=out_specs,
        out_shape=out_shape,
        scratch_shapes=[pltpu.VMEM((2, PEER_TOPK, tb), F32),
                        pltpu.VMEM((2, PEER_TOPK, tb), I32),
                        pltpu.VMEM((PEER_TOPK, tb), F32),
                        pltpu.VMEM((PEER_TOPK, tb), I32),
                        pltpu.VMEM((hk, tb), I32),
                        pltpu.VMEM((hk, tb), F32)],
        compiler_params=_params("arbitrary"),
        name="peer_topk",
    )(*args)


def _stream_copy(src_hbm, buf, sem, step, block):
    slot = lax.rem(step, STREAM_SLOTS)
    return pltpu.make_async_copy(src_hbm.at[pl.ds(block * E_BLOCK, E_BLOCK)], buf.at[slot], sem.at[slot])


def _stream_prefetch(copy_for_step, step, total):
    @pl.when(step == 0)
    def _():
        for k in range(STREAM_AHEAD):
            copy_for_step(k).start()

    @pl.when(step + STREAM_AHEAD < total)
    def _():
        copy_for_step(step + STREAM_AHEAD).start()


def _peer_up_kernel(hn_ref, u_hbm, idx_ref, p_ref, s_even, s_odd, ubuf, usem):
    j = pl.program_id(1)
    nj = pl.num_programs(1)
    step_id = pl.program_id(0) * nj + j
    per = E_BLOCK // N_KEYS

    def u_copy(s):
        return _stream_copy(u_hbm, ubuf, usem, s, jnp.minimum(lax.rem(s, nj), nj - 2))

    _stream_prefetch(u_copy, step_id, pl.num_programs(0) * nj)

    @pl.when(j == 0)
    def _():
        p_ref[...] = jnp.zeros_like(p_ref)
        s_odd[...] = jnp.zeros_like(s_odd)

    u_copy(step_id).wait()
    slot = lax.rem(step_id, STREAM_SLOTS)

    def step(cur, prev):
        cur[...] = _dot_nt(hn_ref[...], ubuf[slot].astype(BF16))
        first_row = (j - 1) * per
        for c in range(TB_UP // SUBLANES):
            rows = slice(c * SUBLANES, (c + 1) * SUBLANES)
            idx = idx_ref[rows, :]
            i1 = idx >> 7
            i2 = idx & (N_KEYS - 1)
            acc = p_ref[rows, :]
            for k in range(per):
                picked = jnp.take_along_axis(prev[rows, k * N_KEYS:(k + 1) * N_KEYS], i2, axis=1)
                acc = jnp.where(i1 == first_row + k, picked, acc)
            p_ref[rows, :] = acc

    @pl.when(j % 2 == 0)
    def _():
        step(s_even, s_odd)

    @pl.when(j % 2 == 1)
    def _():
        step(s_odd, s_even)


def _peer_up(hn2, u, idx):
    t = hn2.shape[0]
    hk = PEER_HEADS * PEER_TOPK
    n_blocks = N_EXPERTS // E_BLOCK
    return pl.pallas_call(
        _peer_up_kernel,
        grid=(t // TB_UP, n_blocks + 1),
        in_specs=[pl.BlockSpec((TB_UP, D_MODEL), lambda i, j: (i, 0)),
                  pl.BlockSpec(memory_space=pl.ANY),
                  pl.BlockSpec((TB_UP, hk), lambda i, j: (i, 0))],
        out_specs=pl.BlockSpec((TB_UP, hk), lambda i, j: (i, 0)),
        out_shape=_sds((t, hk), F32),
        scratch_shapes=[pltpu.VMEM((TB_UP, E_BLOCK), F32), pltpu.VMEM((TB_UP, E_BLOCK), F32),
                        pltpu.VMEM((STREAM_SLOTS, E_BLOCK, D_MODEL), u.dtype),
                        pltpu.SemaphoreType.DMA((STREAM_SLOTS,))],
        compiler_params=_params("arbitrary", "arbitrary"),
        name="peer_up",
    )(hn2, u, idx)


def _peer_down_kernel(p_ref, g_ref, idx_ref, v_hbm, x1_ref, mod_ref, fg_ref, y_ref, a_scr, abuf, vbuf, vsem):
    j = pl.program_id(1)
    nj = pl.num_programs(1)
    step_id = pl.program_id(0) * nj + j

    def v_copy(s):
        return _stream_copy(v_hbm, vbuf, vsem, s, lax.rem(s, nj))

    _stream_prefetch(v_copy, step_id, pl.num_programs(0) * nj)
    v_copy(step_id).wait()

    @pl.when(j == 0)
    def _():
        y_ref[...] = jnp.zeros_like(y_ref)
        a_scr[...] = jax.nn.gelu(p_ref[...]) * g_ref[...]
        key_ids = lax.broadcasted_iota(I32, (N_KEYS, PEER_HEADS * PEER_TOPK), 0)

        def body(tt, carry):
            for u in range(SCATTER_UNROLL):
                t = tt * SCATTER_UNROLL + u
                idx = idx_ref[pl.ds(t, 1), :]
                pm = jnp.where((idx >> 7) == key_ids, a_scr[pl.ds(t, 1), :], 0.0).astype(BF16)
                qm = jnp.where((idx & (N_KEYS - 1)) == key_ids, 1.0, 0.0).astype(BF16)
                abuf[pl.ds(pl.multiple_of(t * A_STRIDE, SUBLANES), N_KEYS), :] = _dot_nt(pm, qm)
            return carry

        lax.fori_loop(0, TB_DOWN // SCATTER_UNROLL, body, 0)

    per = E_BLOCK // N_KEYS
    cols = [abuf[pl.ds(j * per + k, TB_DOWN, stride=A_STRIDE), :] for k in range(per)]
    a_blk = jnp.concatenate(cols, axis=1).astype(BF16)
    y_ref[...] += _dot(a_blk, vbuf[lax.rem(step_id, STREAM_SLOTS)])

    @pl.when(j == nj - 1)
    def _():
        x2 = x1_ref[...] + mod_ref[5:6, :] * y_ref[...]
        y_ref[...] = _rms(x2) * fg_ref[...]


def _peer_down(p, g, idx, v_bf, x1, mod, mod_row, final_g):
    t = p.shape[0]
    hk = PEER_HEADS * PEER_TOPK
    tok = pl.BlockSpec((TB_DOWN, hk), lambda i, j: (i, 0))
    wide = pl.BlockSpec((TB_DOWN, D_MODEL), lambda i, j: (i, 0))
    return pl.pallas_call(
        _peer_down_kernel,
        grid=(t // TB_DOWN, N_EXPERTS // E_BLOCK),
        in_specs=[tok, tok, tok, pl.BlockSpec(memory_space=pl.ANY), wide,
                  pl.BlockSpec((None, N_MOD, D_MODEL), lambda i, j: (mod_row(i * TB_DOWN), 0, 0)),
                  pl.BlockSpec((1, D_MODEL), lambda i, j: (0, 0))],
        out_specs=wide,
        out_shape=_sds((t, D_MODEL), F32),
        scratch_shapes=[pltpu.VMEM((TB_DOWN, hk), F32),
                        pltpu.VMEM((TB_DOWN * A_STRIDE, N_KEYS), F32),
                        pltpu.VMEM((STREAM_SLOTS, E_BLOCK, D_MODEL), v_bf.dtype),
                        pltpu.SemaphoreType.DMA((STREAM_SLOTS,))],
        compiler_params=_params("arbitrary", "arbitrary"),
        name="peer_down",
    )(p, g, idx, v_bf, x1, mod, final_g)


def _grid_pos_embed(n_tokens):
    rows = n_tokens // GRID_W
    r = np.repeat(np.arange(rows, dtype=np.float64), GRID_W)
    col = np.tile(np.arange(GRID_W, dtype=np.float64), rows)
    quarter = D_MODEL // 4
    omega = 1.0 / (POS_THETA ** (np.arange(quarter, dtype=np.float64) / quarter))
    ar = r[:, None] * omega
    ac = col[:, None] * omega
    table = np.concatenate([np.sin(ar), np.cos(ar), np.sin(ac), np.cos(ac)], axis=-1)
    return jnp.asarray(table.astype(np.float32))


def _trunk(x, pos, mod, mod_base, per_seq, s0t, lb, w, v_bf):
    n_seq, seq_len, _ = x.shape
    x2d = x.reshape(n_seq * seq_len, D_MODEL)

    def mod_row(tok):
        return mod_base + (tok // seq_len if per_seq else 0)

    z = _inproj(x2d, pos, mod, mod_row, w["norm1_g"], w["w_in"], seq_len)
    o_fw, o_bw, st_fin = _hgrn(z, lb, s0t, n_seq, seq_len)
    x1, hn2 = _merge(x2d, pos, z, o_fw, o_bw, mod, mod_row, seq_len, w["sgu_norm_g"], w["w_spatial"],
                     w["b_spatial_t"], w["hgrn_norm_g"], w["w_proj_a"], w["w_proj_b"], w["w_out"],
                     w["norm2_g"])
    if v_bf is None:
        idx, gate, v_bf = _topk(hn2, w["peer_w_q"], w["peer_sub_keys"], table=w["peer_v"])
    else:
        idx, gate = _topk(hn2, w["peer_w_q"], w["peer_sub_keys"])
    p = _peer_up(hn2, w["peer_u"], idx)
    y = _peer_down(p, gate, idx, v_bf, x1, mod, mod_row, w["final_norm_g"])
    return y.reshape(x.shape), jnp.swapaxes(st_fin, -1, -2), v_bf


def kernel(x_prompt, x_sample, state_hgrn, c, c_ctx, w_ada, b_ada, norm1_g, w_in, sgu_norm_g,
           w_spatial, b_spatial, hgrn_lb, hgrn_norm_g, w_proj_a, w_proj_b, w_out, norm2_g,
           peer_w_q, peer_sub_keys, peer_u, peer_v, final_norm_g):
    assert w_in.shape[0] == 1, "single-layer stack"
    n_p, l_p, _ = x_prompt.shape
    n_s, l_s, _ = x_sample.shape
    assert n_s + 1 <= SUBLANES
    lb = jnp.cumsum(jax.nn.softmax(hgrn_lb.astype(F32), axis=0), axis=0)[0]

    cond = jnp.zeros((SUBLANES, D_MODEL), F32).at[0].set(c_ctx).at[1:1 + n_s].set(c)
    mod = _adaln(cond, w_ada[0], b_ada[0][None]).reshape(SUBLANES, N_MOD, D_MODEL)

    w = dict(
        norm1_g=norm1_g[0][None], w_in=w_in[0].astype(BF16), sgu_norm_g=sgu_norm_g[0][None],
        w_spatial=w_spatial[0].astype(BF16), b_spatial_t=b_spatial[0].T,
        hgrn_norm_g=hgrn_norm_g[0].reshape(1, H_B * DV),
        w_proj_a=w_proj_a[0].astype(BF16), w_proj_b=w_proj_b[0].astype(BF16),
        w_out=w_out[0].astype(BF16), norm2_g=norm2_g[0][None],
        peer_w_q=peer_w_q[0].astype(BF16),
        peer_sub_keys=peer_sub_keys[0].reshape(2 * PEER_HEADS, N_KEYS, 128).astype(BF16),
        peer_u=peer_u[0], peer_v=peer_v[0],
        final_norm_g=final_norm_g[None],
    )

    s0_ctx = jnp.zeros((n_p, 2, H_B, DV, DK), F32)
    s0_lat = jnp.swapaxes(state_hgrn[:, 0].astype(F32), -1, -2)
    pos = _grid_pos_embed(l_s)

    y_prompt, s_ctx, v_bf = _trunk(x_prompt, None, mod, 0, False, s0_ctx, lb, w, None)
    y_sample, _, _ = _trunk(x_sample, pos, mod, 1, True, s0_lat, lb, w, v_bf)
    return (y_prompt, y_sample, s_ctx[:, None])
```

```python
import functools

import jax
import jax.numpy as jnp
import numpy as np
from jax import lax
from jax.experimental import pallas as pl
from jax.experimental.pallas import tpu as pltpu

F32 = jnp.float32
BF16 = jnp.bfloat16
I32 = jnp.int32

D_MODEL = 1024
N_MOD = 6
A_WIDTH = 512
A_GROUPS = 4
CHUNK_MLP = 128
H_B = 4
DK = 128
DV = 128
QK_WIDTH = H_B * DK
IN_WIDTH = 5632
COL_BLOCK = 512
PEER_HEADS = 8
N_KEYS = 128
PEER_TOPK = 16
PEER_Q = 2 * PEER_HEADS * 128
N_EXPERTS = N_KEYS * N_KEYS
GRID_W = 64
POS_THETA = 10000.0
EPS = 1e-6

SUBLANES = 8
LANES = 128
VMEM_LIMIT_BYTES = 56 * 1024 * 1024

TB_IN = 512
CH = 128
TB_MERGE = 512
TB_TOPK = 256
TB_UP = 1024
TB_DOWN = 512
E_BLOCK = 1024
A_STRIDE = 136
SCATTER_UNROLL = 128
STREAM_AHEAD = 2
STREAM_SLOTS = STREAM_AHEAD + 1


def _sds(shape, dtype):
    return jax.ShapeDtypeStruct(shape, dtype)


def _params(*sem):
    return pltpu.CompilerParams(dimension_semantics=sem, vmem_limit_bytes=VMEM_LIMIT_BYTES)


def _dot(a, b):
    return jnp.dot(a, b, preferred_element_type=F32)


def _dot_nt(a, b):
    return lax.dot_general(a, b, (((1,), (1,)), ((), ())), preferred_element_type=F32)


def _dot_tn(a, b):
    return lax.dot_general(a, b, (((0,), (0,)), ((), ())), preferred_element_type=F32)


def _rms(x):
    return x * lax.rsqrt(jnp.mean(x * x, axis=-1, keepdims=True) + EPS)


def _adaln_kernel(c_ref, w_ref, b_ref, tab_ref, o_ref, tab_out_ref):
    a = jax.nn.silu(c_ref[...])
    o_ref[...] = jnp.dot(a, w_ref[...], preferred_element_type=F32,
                         precision=lax.Precision.HIGHEST) + b_ref[...]
    tab_out_ref[...] = tab_ref[...].astype(BF16)


def _adaln(cond8, w_ada, b_ada, table):
    n = w_ada.shape[1]
    tn = 1536
    steps = n // tn
    rows = table.shape[0] // steps
    assert rows * steps == table.shape[0] and rows % 16 == 0
    tab_spec = pl.BlockSpec((rows, table.shape[1]), lambda j: (j, 0))
    return pl.pallas_call(
        _adaln_kernel,
        grid=(steps,),
        in_specs=[pl.BlockSpec((SUBLANES, D_MODEL), lambda j: (0, 0)),
                  pl.BlockSpec((D_MODEL, tn), lambda j: (0, j)),
                  pl.BlockSpec((1, tn), lambda j: (0, j)),
                  tab_spec],
        out_specs=[pl.BlockSpec((SUBLANES, tn), lambda j: (0, j)), tab_spec],
        out_shape=[_sds((SUBLANES, n), F32), _sds(table.shape, BF16)],
        compiler_params=_params("arbitrary"),
        name="adaln",
    )(cond8, w_ada, b_ada, table)


def _inproj_kernel(*refs, add_pos):
    if add_pos:
        x_ref, pos_ref, mod_ref, g_ref, w_ref, z_ref = refs
    else:
        x_ref, mod_ref, g_ref, w_ref, z_ref = refs
    x = x_ref[...]
    if add_pos:
        x = x + pos_ref[...]
    h = _rms(x) * g_ref[...] * (1.0 + mod_ref[1:2, :]) + mod_ref[0:1, :]
    z_ref[...] = _dot(h.astype(BF16), w_ref[...])


def _inproj(x2d, pos, mod, mod_row, norm_g, w_in_bf, seq_len):
    t = x2d.shape[0]
    bps = seq_len // TB_IN
    add_pos = pos is not None
    in_specs = [pl.BlockSpec((TB_IN, D_MODEL), lambda i: (i, 0))]
    args = [x2d]
    if add_pos:
        in_specs.append(pl.BlockSpec((TB_IN, D_MODEL), lambda i: (i % bps, 0)))
        args.append(pos)
    in_specs += [pl.BlockSpec((None, N_MOD, D_MODEL), lambda i: (mod_row(i * TB_IN), 0, 0)),
                 pl.BlockSpec((1, D_MODEL), lambda i: (0, 0)),
                 pl.BlockSpec((D_MODEL, IN_WIDTH), lambda i: (0, 0), pipeline_mode=pl.Buffered(1))]
    args += [mod, norm_g, w_in_bf]
    return pl.pallas_call(
        functools.partial(_inproj_kernel, add_pos=add_pos),
        grid=(t // TB_IN,),
        in_specs=in_specs,
        out_specs=pl.BlockSpec((TB_IN, IN_WIDTH), lambda i: (i, 0)),
        out_shape=_sds((t, IN_WIDTH), F32),
        compiler_params=_params("arbitrary"),
        name="inproj",
    )(*args)


def _split3(x):
    hi = x.astype(BF16)
    r1 = x - hi.astype(F32)
    mid = r1.astype(BF16)
    lo = (r1 - mid.astype(F32)).astype(BF16)
    return hi, mid, lo


def _hgrn_direction(zq_ref, zf_ref, zv_ref, lb, st_scr, b_scr, k_scr, o_ref, reverse):
    f = lb + (1.0 - lb) * jax.nn.sigmoid(zf_ref[...])
    k_scr[...] = 1.0 - f
    lf = jnp.log(f)
    row = lax.broadcasted_iota(I32, (CH, CH), 0)
    col = lax.broadcasted_iota(I32, (CH, CH), 1)
    tri = jnp.where((row <= col) if reverse else (row >= col), 1.0, 0.0).astype(BF16)
    hi, mid, lo = _split3(lf)
    b_scr[...] = _dot(tri, hi) + _dot(tri, mid) + _dot(tri, lo)

    sub = lax.broadcasted_iota(I32, (SUBLANES, DK), 0)
    zero_tile = jnp.zeros((SUBLANES, DK), F32)
    n_tiles = CH // SUBLANES
    end_row = 0 if reverse else CH - 1

    for h in range(H_B):
        sl = slice(h * DK, (h + 1) * DK)
        q = zq_ref[:, sl]
        kk = k_scr[:, sl]
        bh = b_scr[:, sl]

        def masked(sc, m):
            if 2 * m >= CH:
                return sc
            shift = (2 * m).bit_length() - 1 if m else 0
            return jnp.where((row >> shift) == (col >> shift), sc, 0.0)

        s_tot = masked(_dot_nt(q.astype(BF16), kk.astype(BF16)), 0)

        m = 1
        while m < CH:
            q_tiles, k_tiles = [], []
            for j in range(n_tiles):
                r0 = j * SUBLANES
                rows = slice(r0, r0 + SUBLANES)
                if m < SUBLANES:
                    upper = (sub & m) != 0
                    gives_q = ~upper if reverse else upper
                    if m == 1:
                        scale_q = 1.0 - k_scr[rows, sl]
                        scale_k = None
                    else:
                        seam = m if reverse else m - 1
                        if 2 * m == SUBLANES:
                            b_mid = b_scr[r0 + seam:r0 + seam + 1, sl]
                        else:
                            b_mid = jnp.where(sub < 2 * m, b_scr[r0 + seam:r0 + seam + 1, sl],
                                              b_scr[r0 + 2 * m + seam:r0 + 2 * m + seam + 1, sl])
                        d = b_scr[rows, sl] - b_mid
                        scale_q = scale_k = jnp.exp(jnp.minimum(d, -d))
                    qt = zq_ref[rows, sl] * scale_q
                    kt = k_scr[rows, sl] if scale_k is None else k_scr[rows, sl] * scale_k
                    q_tiles.append(jnp.where(gives_q, qt, 0.0))
                    k_tiles.append(jnp.where(gives_q, 0.0, kt))
                else:
                    blk = r0 // m
                    gives_q = (blk % 2 == 0) if reverse else (blk % 2 == 1)
                    if gives_q:
                        ref_row = blk * m + m if reverse else blk * m - 1
                        q_tiles.append(zq_ref[rows, sl] * jnp.exp(
                            b_scr[rows, sl] - b_scr[ref_row:ref_row + 1, sl]))
                        k_tiles.append(zero_tile)
                    else:
                        ref_row = blk * m if reverse else blk * m + m - 1
                        k_tiles.append(k_scr[rows, sl] * jnp.exp(
                            b_scr[ref_row:ref_row + 1, sl] - b_scr[rows, sl]))
                        q_tiles.append(zero_tile)
            qm = jnp.concatenate(q_tiles, axis=0).astype(BF16)
            km = jnp.concatenate(k_tiles, axis=0).astype(BF16)
            s_tot = s_tot + masked(_dot_nt(qm, km), m)
            m *= 2

        v = zv_ref[:, sl].astype(BF16)
        st = st_scr[h]
        o_ref[:, sl] = (_dot(s_tot.astype(BF16), v)
                        + _dot_nt((q * jnp.exp(bh)).astype(BF16), st.astype(BF16)))
        b_end = b_scr[end_row:end_row + 1, sl]
        ke = (kk * jnp.exp(b_end - bh)).astype(BF16)
        st_scr[h] = st * jnp.exp(b_end) + _dot_tn(v, ke)


def _hgrn_kernel(zqf_ref, zff_ref, zvf_ref, zqb_ref, zfb_ref, zvb_ref, lb_ref, s0_ref,
                 ofw_ref, obw_ref, sfin_ref, st_scr, b_scr, k_scr, *, n_chunk):
    c = pl.program_id(1)

    @pl.when(c == 0)
    def _():
        st_scr[...] = s0_ref[...]

    _hgrn_direction(zqf_ref, zff_ref, zvf_ref, lb_ref[0:1, :], st_scr.at[0], b_scr.at[0],
                    k_scr.at[0], ofw_ref, reverse=False)
    _hgrn_direction(zqb_ref, zfb_ref, zvb_ref, lb_ref[1:2, :], st_scr.at[1], b_scr.at[1],
                    k_scr.at[1], obw_ref, reverse=True)

    @pl.when(c == n_chunk - 1)
    def _():
        sfin_ref[...] = st_scr[...]


def _hgrn(z, lb, s0t, n_seq, seq_len):
    n_chunk = seq_len // CH
    t = n_seq * seq_len

    def fwd(col):
        return pl.BlockSpec((CH, COL_BLOCK), lambda s, c: (s * n_chunk + c, col))

    def bwd(col):
        return pl.BlockSpec((CH, COL_BLOCK), lambda s, c: (s * n_chunk + n_chunk - 1 - c, col))

    state_spec = pl.BlockSpec((None, 2, H_B, DV, DK), lambda s, c: (s, 0, 0, 0, 0))
    return pl.pallas_call(
        functools.partial(_hgrn_kernel, n_chunk=n_chunk),
        grid=(n_seq, n_chunk),
        in_specs=[fwd(2), fwd(3), fwd(5), bwd(2), bwd(4), bwd(5),
                  pl.BlockSpec((2, QK_WIDTH), lambda s, c: (0, 0)), state_spec],
        out_specs=[fwd(0), bwd(0), state_spec],
        out_shape=[_sds((t, H_B * DV), F32), _sds((t, H_B * DV), F32),
                   _sds((n_seq, 2, H_B, DV, DK), F32)],
        scratch_shapes=[pltpu.VMEM((2, H_B, DV, DK), F32),
                        pltpu.VMEM((2, CH, QK_WIDTH), F32),
                        pltpu.VMEM((2, CH, QK_WIDTH), F32)],
        compiler_params=_params("arbitrary", "arbitrary"),
        name="hgrn",
    )(z, z, z, z, z, z, lb, s0t)


def _merge_kernel(*refs, add_pos):
    refs = list(refs)
    x_ref = refs.pop(0)
    pos_ref = refs.pop(0) if add_pos else None
    (zu_ref, zv_ref, zg_ref, za0_ref, za1_ref, zb0_ref, zb1_ref, ofw_ref, obw_ref, mod_ref,
     sgug_ref, ws_ref, bst_ref, hng_ref, wpa_ref, wpb_ref, wout_ref, n2g_ref,
     x1_ref, hn2_ref) = refs

    u = jax.nn.gelu(zu_ref[...])
    v = (_rms(jax.nn.gelu(zv_ref[...])) * sgug_ref[...]).astype(BF16)
    row_parts = []
    for ci in range(TB_MERGE // CHUNK_MLP):
        col_parts = []
        for g in range(A_GROUPS):
            vg = v[ci * CHUNK_MLP:(ci + 1) * CHUNK_MLP, g * 128:(g + 1) * 128]
            col_parts.append(_dot(ws_ref[g], vg) + bst_ref[:, g:g + 1])
        row_parts.append(jnp.concatenate(col_parts, axis=1))
    y_a = u * jnp.concatenate(row_parts, axis=0)

    o = ofw_ref[...] + obw_ref[...]
    on = jnp.concatenate([_rms(o[:, h * DV:(h + 1) * DV]) for h in range(H_B)], axis=1)
    y_b = on * hng_ref[...] * jax.nn.silu(zg_ref[...])

    pa = _dot(y_a.astype(BF16), wpa_ref[...])
    pb = _dot(y_b.astype(BF16), wpb_ref[...])
    za = jnp.concatenate([za0_ref[...], za1_ref[...]], axis=1)
    zb = jnp.concatenate([zb0_ref[...], zb1_ref[...]], axis=1)
    mix_in = jax.nn.sigmoid(za) * pa + jax.nn.sigmoid(zb) * pb
    mix = _dot(mix_in.astype(BF16), wout_ref[...])

    x = x_ref[...]
    if add_pos:
        x = x + pos_ref[...]
    x1 = x + mod_ref[2:3, :] * mix
    x1_ref[...] = x1
    hn2 = _rms(x1) * n2g_ref[...] * (1.0 + mod_ref[4:5, :]) + mod_ref[3:4, :]
    hn2_ref[...] = hn2.astype(BF16)


def _merge(x2d, pos, z, o_fw, o_bw, mod, mod_row, seq_len, sgu_g, ws_bf, bs_t, hgrn_g,
           wpa_bf, wpb_bf, wout_bf, norm2_g):
    t = x2d.shape[0]
    tb = TB_MERGE
    bps = seq_len // tb
    add_pos = pos is not None

    def zcol(k):
        return pl.BlockSpec((tb, COL_BLOCK), lambda i: (i, k))

    def full(shape):
        return pl.BlockSpec(shape, lambda i: (0,) * len(shape))

    in_specs = [pl.BlockSpec((tb, D_MODEL), lambda i: (i, 0))]
    args = [x2d]
    if add_pos:
        in_specs.append(pl.BlockSpec((tb, D_MODEL), lambda i: (i % bps, 0)))
        args.append(pos)
    in_specs += [zcol(0), zcol(1), zcol(6), zcol(7), zcol(8), zcol(9), zcol(10),
                 pl.BlockSpec((tb, A_WIDTH), lambda i: (i, 0)),
                 pl.BlockSpec((tb, A_WIDTH), lambda i: (i, 0)),
                 pl.BlockSpec((None, N_MOD, D_MODEL), lambda i: (mod_row(i * tb), 0, 0)),
                 full((1, A_WIDTH)), full((A_GROUPS, CHUNK_MLP, CHUNK_MLP)),
                 full((CHUNK_MLP, A_GROUPS)), full((1, H_B * DV)),
                 full((A_WIDTH, D_MODEL)), full((H_B * DV, D_MODEL)),
                 full((D_MODEL, D_MODEL)), full((1, D_MODEL))]
    args += [z] * 7 + [o_fw, o_bw, mod, sgu_g, ws_bf, bs_t, hgrn_g, wpa_bf, wpb_bf, wout_bf, norm2_g]
    return pl.pallas_call(
        functools.partial(_merge_kernel, add_pos=add_pos),
        grid=(t // tb,),
        in_specs=in_specs,
        out_specs=[pl.BlockSpec((tb, D_MODEL), lambda i: (i, 0)),
                   pl.BlockSpec((tb, D_MODEL), lambda i: (i, 0))],
        out_shape=[_sds((t, D_MODEL), F32), _sds((t, D_MODEL), BF16)],
        compiler_params=_params("arbitrary"),
        name="merge",
    )(*args)


def _sort_network(n):
    def merge(lo, hi, r):
        step = r * 2
        if step < hi - lo:
            yield from merge(lo, hi, step)
            yield from merge(lo + r, hi, step)
            yield from [(i, i + r) for i in range(lo + r, hi - r, step)]
        else:
            yield (lo, lo + r)

    def sort(lo, hi):
        if hi - lo >= 1:
            mid = lo + (hi - lo) // 2
            yield from sort(lo, mid)
            yield from sort(mid + 1, hi)
            yield from merge(lo, hi, 1)

    return tuple(sort(0, n - 1))


def _col_max(x):
    return jnp.max(x, axis=0, keepdims=True)


def _col_min(x):
    return jnp.min(x, axis=0, keepdims=True)


def _topk_kernel(*refs, cast_table):
    if cast_table:
        (hn_ref, wq_ref, keys_ref, tab_ref, idx_ref, g_ref, tab_out_ref,
         v_scr, i_scr, tv_scr, fp_scr, idxt_scr, gt_scr) = refs
        tab_out_ref[...] = tab_ref[...].astype(BF16)
    else:
        (hn_ref, wq_ref, keys_ref, idx_ref, g_ref,
         v_scr, i_scr, tv_scr, fp_scr, idxt_scr, gt_scr) = refs
    tb = hn_ref.shape[0]
    k = PEER_TOPK
    q = _dot(hn_ref[...], wq_ref[...]).astype(BF16)
    sub = lax.broadcasted_iota(I32, (SUBLANES, tb), 0).astype(F32)
    n_tiles = N_KEYS // SUBLANES
    neg = -jnp.inf

    for h in range(PEER_HEADS):
        for p in range(2):
            hp = 2 * h + p
            s = _dot_nt(keys_ref[hp], q[:, hp * 128:(hp + 1) * 128])
            vals = [s[SUBLANES * v:SUBLANES * (v + 1), :] for v in range(n_tiles)]
            ids = [sub + float(SUBLANES * v) for v in range(n_tiles)]
            for i, j in _sort_network(n_tiles):
                a, b, ia, ib = vals[i], vals[j], ids[i], ids[j]
                swap = (b > a) | ((b == a) & (ib < ia))
                vals[i], vals[j] = jnp.where(swap, b, a), jnp.where(swap, a, b)
                ids[i], ids[j] = jnp.where(swap, ib, ia), jnp.where(swap, ia, ib)
            for r in range(k):
                m = _col_max(vals[0])
                first = _col_min(jnp.where(vals[0] == m, ids[0], float(N_KEYS)))
                sel = ids[0] == first
                v_scr[p, r:r + 1, :] = m
                i_scr[p, r:r + 1, :] = first
                left = k - 1 - r
                for d in range(left):
                    vals[d] = jnp.where(sel, vals[d + 1], vals[d])
                    ids[d] = jnp.where(sel, ids[d + 1], ids[d])

        lists = []
        for r in range(k):
            val = v_scr[0, r:r + 1, :] + v_scr[1, 0:SUBLANES, :]
            n_valid = k // (r + 1)
            lists.append(val if n_valid >= SUBLANES else jnp.where(sub < float(n_valid), val, neg))
        extra = v_scr[0, 0:1, :] + v_scr[1, SUBLANES:k, :]
        flat = sub
        flat_extra = sub + float(SUBLANES)
        for j in range(k):
            m = _col_max(jnp.maximum(lists[0], extra))
            first = _col_min(jnp.minimum(jnp.where(lists[0] == m, flat, float(k * k)),
                                         jnp.where(extra == m, flat_extra, float(k * k))))
            tv_scr[j:j + 1, :] = m
            fp_scr[j:j + 1, :] = first
            left = k - 1 - j
            if left:
                sel = flat == first
                for d in range(left):
                    lists[d] = jnp.where(sel, lists[d + 1], lists[d])
                flat = jnp.where(sel, flat + float(k), flat)
                extra = jnp.where(flat_extra == first, neg, extra)

        fp = fp_scr[...]
        pr = jnp.floor(fp * (1.0 / k))
        pc = fp - pr * float(k)
        e1 = jnp.zeros_like(fp)
        e2 = jnp.zeros_like(fp)
        for r in range(k):
            e1 = jnp.where(pr == float(r), i_scr[0, r:r + 1, :], e1)
            e2 = jnp.where(pc == float(r), i_scr[1, r:r + 1, :], e2)
        idxt_scr[h * k:(h + 1) * k, :] = (e1 * float(N_KEYS) + e2).astype(I32)
        tv = tv_scr[...]
        ex = jnp.exp(tv - tv[0:1, :])
        gt_scr[h * k:(h + 1) * k, :] = ex / jnp.sum(ex, axis=0, keepdims=True)

    idx_ref[...] = idxt_scr[...].T
    g_ref[...] = gt_scr[...].T


def _topk(hn2, wq_bf, keys_bf, table=None):
    t = hn2.shape[0]
    tb = TB_TOPK
    hk = PEER_HEADS * PEER_TOPK
    steps = t // tb
    in_specs = [pl.BlockSpec((tb, D_MODEL), lambda i: (i, 0)),
                pl.BlockSpec((D_MODEL, PEER_Q), lambda i: (0, 0)),
                pl.BlockSpec((2 * PEER_HEADS, N_KEYS, 128), lambda i: (0, 0, 0))]
    out_specs = [pl.BlockSpec((tb, hk), lambda i: (i, 0)), pl.BlockSpec((tb, hk), lambda i: (i, 0))]
    out_shape = [_sds((t, hk), I32), _sds((t, hk), F32)]
    args = [hn2, wq_bf, keys_bf]
    if table is not None:
        rows = table.shape[0] // steps
        assert rows * steps == table.shape[0] and rows % 16 == 0
        in_specs.append(pl.BlockSpec((rows, table.shape[1]), lambda i: (i, 0)))
        out_specs.append(pl.BlockSpec((rows, table.shape[1]), lambda i: (i, 0)))
        out_shape.append(_sds(table.shape, BF16))
        args.append(table)
    return pl.pallas_call(
        functools.partial(_topk_kernel, cast_table=table is not None),
        grid=(steps,),
        in_specs=in_specs,
        out_specs=out_specs,
        out_shape=out_shape,
        scratch_shapes=[pltpu.VMEM((2, PEER_TOPK, tb), F32),
                        pltpu.VMEM((2, PEER_TOPK, tb), F32),
                        pltpu.VMEM((PEER_TOPK, tb), F32),
                        pltpu.VMEM((PEER_TOPK, tb), F32),
                        pltpu.VMEM((hk, tb), I32),
                        pltpu.VMEM((hk, tb), F32)],
        compiler_params=_params("arbitrary"),
        name="peer_topk",
    )(*args)


def _stream_copy(src_hbm, buf, sem, step, block):
    slot = lax.rem(step, STREAM_SLOTS)
    return pltpu.make_async_copy(src_hbm.at[pl.ds(block * E_BLOCK, E_BLOCK)], buf.at[slot], sem.at[slot])


def _stream_prefetch(copy_for_step, step, total):
    @pl.when(step == 0)
    def _():
        for k in range(STREAM_AHEAD):
            copy_for_step(k).start()

    @pl.when(step + STREAM_AHEAD < total)
    def _():
        copy_for_step(step + STREAM_AHEAD).start()


def _peer_up_kernel(hn_ref, u_hbm, idx_ref, p_ref, s_even, s_odd, ubuf, usem):
    j = pl.program_id(1)
    nj = pl.num_programs(1)
    step_id = pl.program_id(0) * nj + j
    per = E_BLOCK // N_KEYS

    def u_copy(s):
        return _stream_copy(u_hbm, ubuf, usem, s, jnp.minimum(lax.rem(s, nj), nj - 2))

    _stream_prefetch(u_copy, step_id, pl.num_programs(0) * nj)

    @pl.when(j == 0)
    def _():
        p_ref[...] = jnp.zeros_like(p_ref)
        s_odd[...] = jnp.zeros_like(s_odd)

    u_copy(step_id).wait()
    slot = lax.rem(step_id, STREAM_SLOTS)

    def step(cur, prev):
        cur[...] = _dot_nt(hn_ref[...], ubuf[slot].astype(BF16))
        first_row = (j - 1) * per
        for c in range(TB_UP // SUBLANES):
            rows = slice(c * SUBLANES, (c + 1) * SUBLANES)
            idx = idx_ref[rows, :]
            i1 = idx >> 7
            i2 = idx & (N_KEYS - 1)
            acc = p_ref[rows, :]
            for k in range(per):
                picked = jnp.take_along_axis(prev[rows, k * N_KEYS:(k + 1) * N_KEYS], i2, axis=1)
                acc = jnp.where(i1 == first_row + k, picked, acc)
            p_ref[rows, :] = acc

    @pl.when(j % 2 == 0)
    def _():
        step(s_even, s_odd)

    @pl.when(j % 2 == 1)
    def _():
        step(s_odd, s_even)


def _peer_up(hn2, u, idx):
    t = hn2.shape[0]
    hk = PEER_HEADS * PEER_TOPK
    n_blocks = N_EXPERTS // E_BLOCK
    return pl.pallas_call(
        _peer_up_kernel,
        grid=(t // TB_UP, n_blocks + 1),
        in_specs=[pl.BlockSpec((TB_UP, D_MODEL), lambda i, j: (i, 0)),
                  pl.BlockSpec(memory_space=pl.ANY),
                  pl.BlockSpec((TB_UP, hk), lambda i, j: (i, 0))],
        out_specs=pl.BlockSpec((TB_UP, hk), lambda i, j: (i, 0)),
        out_shape=_sds((t, hk), F32),
        scratch_shapes=[pltpu.VMEM((TB_UP, E_BLOCK), F32), pltpu.VMEM((TB_UP, E_BLOCK), F32),
                        pltpu.VMEM((STREAM_SLOTS, E_BLOCK, D_MODEL), u.dtype),
                        pltpu.SemaphoreType.DMA((STREAM_SLOTS,))],
        compiler_params=_params("arbitrary", "arbitrary"),
        name="peer_up",
    )(hn2, u, idx)


def _peer_down_kernel(p_ref, g_ref, idx_ref, v_hbm, x1_ref, mod_ref, fg_ref, y_ref, a_scr, abuf, vbuf, vsem):
    j = pl.program_id(1)
    nj = pl.num_programs(1)
    step_id = pl.program_id(0) * nj + j

    def v_copy(s):
        return _stream_copy(v_hbm, vbuf, vsem, s, lax.rem(s, nj))

    _stream_prefetch(v_copy, step_id, pl.num_programs(0) * nj)
    v_copy(step_id).wait()

    @pl.when(j == 0)
    def _():
        y_ref[...] = jnp.zeros_like(y_ref)
        a_scr[...] = jax.nn.gelu(p_ref[...]) * g_ref[...]
        key_ids = lax.broadcasted_iota(I32, (N_KEYS, PEER_HEADS * PEER_TOPK), 0)

        def body(tt, carry):
            for u in range(SCATTER_UNROLL):
                t = tt * SCATTER_UNROLL + u
                idx = idx_ref[pl.ds(t, 1), :]
                pm = jnp.where((idx >> 7) == key_ids, a_scr[pl.ds(t, 1), :], 0.0).astype(BF16)
                qm = jnp.where((idx & (N_KEYS - 1)) == key_ids, 1.0, 0.0).astype(BF16)
                abuf[pl.ds(pl.multiple_of(t * A_STRIDE, SUBLANES), N_KEYS), :] = _dot_nt(pm, qm)
            return carry

        lax.fori_loop(0, TB_DOWN // SCATTER_UNROLL, body, 0)

    per = E_BLOCK // N_KEYS
    cols = [abuf[pl.ds(j * per + k, TB_DOWN, stride=A_STRIDE), :] for k in range(per)]
    a_blk = jnp.concatenate(cols, axis=1).astype(BF16)
    y_ref[...] += _dot(a_blk, vbuf[lax.rem(step_id, STREAM_SLOTS)])

    @pl.when(j == nj - 1)
    def _():
        x2 = x1_ref[...] + mod_ref[5:6, :] * y_ref[...]
        y_ref[...] = _rms(x2) * fg_ref[...]


def _peer_down(p, g, idx, v_bf, x1, mod, mod_row, final_g):
    t = p.shape[0]
    hk = PEER_HEADS * PEER_TOPK
    tok = pl.BlockSpec((TB_DOWN, hk), lambda i, j: (i, 0))
    wide = pl.BlockSpec((TB_DOWN, D_MODEL), lambda i, j: (i, 0))
    return pl.pallas_call(
        _peer_down_kernel,
        grid=(t // TB_DOWN, N_EXPERTS // E_BLOCK),
        in_specs=[tok, tok, tok, pl.BlockSpec(memory_space=pl.ANY), wide,
                  pl.BlockSpec((None, N_MOD, D_MODEL), lambda i, j: (mod_row(i * TB_DOWN), 0, 0)),
                  pl.BlockSpec((1, D_MODEL), lambda i, j: (0, 0))],
        out_specs=wide,
        out_shape=_sds((t, D_MODEL), F32),
        scratch_shapes=[pltpu.VMEM((TB_DOWN, hk), F32),
                        pltpu.VMEM((TB_DOWN * A_STRIDE, N_KEYS), F32),
                        pltpu.VMEM((STREAM_SLOTS, E_BLOCK, D_MODEL), v_bf.dtype),
                        pltpu.SemaphoreType.DMA((STREAM_SLOTS,))],
        compiler_params=_params("arbitrary", "arbitrary"),
        name="peer_down",
    )(p, g, idx, v_bf, x1, mod, final_g)


def _grid_pos_embed(n_tokens):
    rows = n_tokens // GRID_W
    r = np.repeat(np.arange(rows, dtype=np.float64), GRID_W)
    col = np.tile(np.arange(GRID_W, dtype=np.float64), rows)
    quarter = D_MODEL // 4
    omega = 1.0 / (POS_THETA ** (np.arange(quarter, dtype=np.float64) / quarter))
    ar = r[:, None] * omega
    ac = col[:, None] * omega
    table = np.concatenate([np.sin(ar), np.cos(ar), np.sin(ac), np.cos(ac)], axis=-1)
    return jnp.asarray(table.astype(np.float32))


def _trunk(x, pos, mod, mod_base, per_seq, s0t, lb, w, v_bf):
    n_seq, seq_len, _ = x.shape
    x2d = x.reshape(n_seq * seq_len, D_MODEL)

    def mod_row(tok):
        return mod_base + (tok // seq_len if per_seq else 0)

    z = _inproj(x2d, pos, mod, mod_row, w["norm1_g"], w["w_in"], seq_len)
    o_fw, o_bw, st_fin = _hgrn(z, lb, s0t, n_seq, seq_len)
    x1, hn2 = _merge(x2d, pos, z, o_fw, o_bw, mod, mod_row, seq_len, w["sgu_norm_g"], w["w_spatial"],
                     w["b_spatial_t"], w["hgrn_norm_g"], w["w_proj_a"], w["w_proj_b"], w["w_out"],
                     w["norm2_g"])
    if v_bf is None:
        idx, gate, v_bf = _topk(hn2, w["peer_w_q"], w["peer_sub_keys"], table=w["peer_v"])
    else:
        idx, gate = _topk(hn2, w["peer_w_q"], w["peer_sub_keys"])
    p = _peer_up(hn2, w["peer_u"], idx)
    y = _peer_down(p, gate, idx, v_bf, x1, mod, mod_row, w["final_norm_g"])
    return y.reshape(x.shape), jnp.swapaxes(st_fin, -1, -2), v_bf


def kernel(x_prompt, x_sample, state_hgrn, c, c_ctx, w_ada, b_ada, norm1_g, w_in, sgu_norm_g,
           w_spatial, b_spatial, hgrn_lb, hgrn_norm_g, w_proj_a, w_proj_b, w_out, norm2_g,
           peer_w_q, peer_sub_keys, peer_u, peer_v, final_norm_g):
    assert w_in.shape[0] == 1, "single-layer stack"
    n_p, l_p, _ = x_prompt.shape
    n_s, l_s, _ = x_sample.shape
    assert n_s + 1 <= SUBLANES
    lb = jnp.cumsum(jax.nn.softmax(hgrn_lb.astype(F32), axis=0), axis=0)[0]

    cond = jnp.zeros((SUBLANES, D_MODEL), F32).at[0].set(c_ctx).at[1:1 + n_s].set(c)
    mod, w_in_bf = _adaln(cond, w_ada[0], b_ada[0][None], w_in[0])
    mod = mod.reshape(SUBLANES, N_MOD, D_MODEL)

    w = dict(
        norm1_g=norm1_g[0][None], w_in=w_in_bf, sgu_norm_g=sgu_norm_g[0][None],
        w_spatial=w_spatial[0].astype(BF16), b_spatial_t=b_spatial[0].T,
        hgrn_norm_g=hgrn_norm_g[0].reshape(1, H_B * DV),
        w_proj_a=w_proj_a[0].astype(BF16), w_proj_b=w_proj_b[0].astype(BF16),
        w_out=w_out[0].astype(BF16), norm2_g=norm2_g[0][None],
        peer_w_q=peer_w_q[0].astype(BF16),
        peer_sub_keys=peer_sub_keys[0].reshape(2 * PEER_HEADS, N_KEYS, 128).astype(BF16),
        peer_u=peer_u[0], peer_v=peer_v[0],
        final_norm_g=final_norm_g[None],
    )

    s0_ctx = jnp.zeros((n_p, 2, H_B, DV, DK), F32)
    s0_lat = jnp.swapaxes(state_hgrn[:, 0].astype(F32), -1, -2)
    pos = _grid_pos_embed(l_s)

    y_prompt, s_ctx, v_bf = _trunk(x_prompt, None, mod, 0, False, s0_ctx, lb, w, None)
    y_sample, _, _ = _trunk(x_sample, pos, mod, 1, True, s0_lat, lb, w, v_bf)
    return (y_prompt, y_sample, s_ctx[:, None])
```

```python
import functools

import jax
import jax.numpy as jnp
import numpy as np
from jax import lax
from jax.experimental import pallas as pl
from jax.experimental.pallas import tpu as pltpu

F32 = jnp.float32
BF16 = jnp.bfloat16
I32 = jnp.int32

D_MODEL = 1024
N_MOD = 6
A_WIDTH = 512
A_GROUPS = 4
A_GROUP_DIM = A_WIDTH // A_GROUPS
CHUNK_MLP = 128
H_B = 4
DK = 128
DV = 128
QK_WIDTH = H_B * DK
IN_WIDTH = 5632
COL_BLOCK = 512
PEER_HEADS = 8
N_KEYS = 128
KEY_BITS = 7
PEER_TOPK = 16
PEER_QHALF = 128
PEER_Q = 2 * PEER_HEADS * PEER_QHALF
N_EXPERTS = N_KEYS * N_KEYS
GRID_W = 64
POS_THETA = 10000.0
EPS = 1e-6

SUBLANES = 8
LANES = 128
VMEM_LIMIT_BYTES = 56 * 1024 * 1024

TB_IN = 512
CH = 128
TB_MERGE = 512
TB_TOPK = 256
TB_UP = 1024
TB_DOWN = 512
E_BLOCK = 1024
A_STRIDE = 136
SCATTER_UNROLL = 128
STREAM_AHEAD = 2
STREAM_SLOTS = STREAM_AHEAD + 1


def _sds(shape, dtype):
    return jax.ShapeDtypeStruct(shape, dtype)


def _params(*sem):
    return pltpu.CompilerParams(dimension_semantics=sem, vmem_limit_bytes=VMEM_LIMIT_BYTES)


def _dot(a, b):
    return jnp.dot(a, b, preferred_element_type=F32)


def _dot_nt(a, b):
    return lax.dot_general(a, b, (((1,), (1,)), ((), ())), preferred_element_type=F32)


def _dot_tn(a, b):
    return lax.dot_general(a, b, (((0,), (0,)), ((), ())), preferred_element_type=F32)


def _rms(x):
    return x * lax.rsqrt(jnp.mean(x * x, axis=-1, keepdims=True) + EPS)


def _adaln_kernel(c_ref, w_ref, b_ref, tab_ref, o_ref, tab_out_ref):
    a = jax.nn.silu(c_ref[...])
    o_ref[...] = jnp.dot(a, w_ref[...], preferred_element_type=F32,
                         precision=lax.Precision.HIGHEST) + b_ref[...]
    tab_out_ref[...] = tab_ref[...].astype(BF16)


def _adaln(cond8, w_ada, b_ada, table):
    n = w_ada.shape[1]
    tn = 1536
    steps = n // tn
    rows = table.shape[0] // steps
    assert rows * steps == table.shape[0] and rows % 16 == 0
    tab_spec = pl.BlockSpec((rows, table.shape[1]), lambda j: (j, 0))
    return pl.pallas_call(
        _adaln_kernel,
        grid=(steps,),
        in_specs=[pl.BlockSpec((SUBLANES, D_MODEL), lambda j: (0, 0)),
                  pl.BlockSpec((D_MODEL, tn), lambda j: (0, j)),
                  pl.BlockSpec((1, tn), lambda j: (0, j)),
                  tab_spec],
        out_specs=[pl.BlockSpec((SUBLANES, tn), lambda j: (0, j)), tab_spec],
        out_shape=[_sds((SUBLANES, n), F32), _sds(table.shape, BF16)],
        compiler_params=_params("arbitrary"),
        name="adaln",
    )(cond8, w_ada, b_ada, table)


def _inproj_kernel(*refs, add_pos):
    if add_pos:
        x_ref, pos_ref, mod_ref, g_ref, w_ref, z_ref = refs
    else:
        x_ref, mod_ref, g_ref, w_ref, z_ref = refs
    x = x_ref[...]
    if add_pos:
        x = x + pos_ref[...]
    h = _rms(x) * g_ref[...] * (1.0 + mod_ref[1:2, :]) + mod_ref[0:1, :]
    z_ref[...] = _dot(h.astype(BF16), w_ref[...])


def _inproj(x2d, pos, mod, mod_row, norm_g, w_in_bf, seq_len):
    t = x2d.shape[0]
    bps = seq_len // TB_IN
    add_pos = pos is not None
    in_specs = [pl.BlockSpec((TB_IN, D_MODEL), lambda i: (i, 0))]
    args = [x2d]
    if add_pos:
        in_specs.append(pl.BlockSpec((TB_IN, D_MODEL), lambda i: (i % bps, 0)))
        args.append(pos)
    in_specs += [pl.BlockSpec((None, N_MOD, D_MODEL), lambda i: (mod_row(i * TB_IN), 0, 0)),
                 pl.BlockSpec((1, D_MODEL), lambda i: (0, 0)),
                 pl.BlockSpec((D_MODEL, IN_WIDTH), lambda i: (0, 0), pipeline_mode=pl.Buffered(1))]
    args += [mod, norm_g, w_in_bf]
    return pl.pallas_call(
        functools.partial(_inproj_kernel, add_pos=add_pos),
        grid=(t // TB_IN,),
        in_specs=in_specs,
        out_specs=pl.BlockSpec((TB_IN, IN_WIDTH), lambda i: (i, 0)),
        out_shape=_sds((t, IN_WIDTH), F32),
        compiler_params=_params("arbitrary"),
        name="inproj",
    )(*args)


def _split3(x):
    hi = x.astype(BF16)
    r1 = x - hi.astype(F32)
    mid = r1.astype(BF16)
    lo = (r1 - mid.astype(F32)).astype(BF16)
    return hi, mid, lo


def _hgrn_direction(zq_ref, zf_ref, zv_ref, lb, st_scr, b_scr, k_scr, o_ref, reverse):
    f = lb + (1.0 - lb) * jax.nn.sigmoid(zf_ref[...])
    k_scr[...] = 1.0 - f
    lf = jnp.log(f)
    row = lax.broadcasted_iota(I32, (CH, CH), 0)
    col = lax.broadcasted_iota(I32, (CH, CH), 1)
    tri = jnp.where((row <= col) if reverse else (row >= col), 1.0, 0.0).astype(BF16)
    hi, mid, lo = _split3(lf)
    b_scr[...] = _dot(tri, hi) + _dot(tri, mid) + _dot(tri, lo)

    sub = lax.broadcasted_iota(I32, (SUBLANES, DK), 0)
    zero_tile = jnp.zeros((SUBLANES, DK), F32)
    n_tiles = CH // SUBLANES
    end_row = 0 if reverse else CH - 1

    for h in range(H_B):
        sl = slice(h * DK, (h + 1) * DK)
        q = zq_ref[:, sl]
        kk = k_scr[:, sl]
        bh = b_scr[:, sl]

        def masked(sc, m):
            if 2 * m >= CH:
                return sc
            shift = (2 * m).bit_length() - 1 if m else 0
            return jnp.where((row >> shift) == (col >> shift), sc, 0.0)

        s_tot = masked(_dot_nt(q.astype(BF16), kk.astype(BF16)), 0)

        m = 1
        while m < CH:
            q_tiles, k_tiles = [], []
            for j in range(n_tiles):
                r0 = j * SUBLANES
                rows = slice(r0, r0 + SUBLANES)
                if m < SUBLANES:
                    upper = (sub & m) != 0
                    gives_q = ~upper if reverse else upper
                    if m == 1:
                        scale_q = 1.0 - k_scr[rows, sl]
                        scale_k = None
                    else:
                        seam = m if reverse else m - 1
                        if 2 * m == SUBLANES:
                            b_mid = b_scr[r0 + seam:r0 + seam + 1, sl]
                        else:
                            b_mid = jnp.where(sub < 2 * m, b_scr[r0 + seam:r0 + seam + 1, sl],
                                              b_scr[r0 + 2 * m + seam:r0 + 2 * m + seam + 1, sl])
                        d = b_scr[rows, sl] - b_mid
                        scale_q = scale_k = jnp.exp(jnp.minimum(d, -d))
                    qt = zq_ref[rows, sl] * scale_q
                    kt = k_scr[rows, sl] if scale_k is None else k_scr[rows, sl] * scale_k
                    q_tiles.append(jnp.where(gives_q, qt, 0.0))
                    k_tiles.append(jnp.where(gives_q, 0.0, kt))
                else:
                    blk = r0 // m
                    gives_q = (blk % 2 == 0) if reverse else (blk % 2 == 1)
                    if gives_q:
                        ref_row = blk * m + m if reverse else blk * m - 1
                        q_tiles.append(zq_ref[rows, sl] * jnp.exp(
                            b_scr[rows, sl] - b_scr[ref_row:ref_row + 1, sl]))
                        k_tiles.append(zero_tile)
                    else:
                        ref_row = blk * m if reverse else blk * m + m - 1
                        k_tiles.append(k_scr[rows, sl] * jnp.exp(
                            b_scr[ref_row:ref_row + 1, sl] - b_scr[rows, sl]))
                        q_tiles.append(zero_tile)
            qm = jnp.concatenate(q_tiles, axis=0).astype(BF16)
            km = jnp.concatenate(k_tiles, axis=0).astype(BF16)
            s_tot = s_tot + masked(_dot_nt(qm, km), m)
            m *= 2

        v = zv_ref[:, sl].astype(BF16)
        st = st_scr[h]
        o_ref[:, sl] = (_dot(s_tot.astype(BF16), v)
                        + _dot_nt((q * jnp.exp(bh)).astype(BF16), st.astype(BF16)))
        b_end = b_scr[end_row:end_row + 1, sl]
        ke = (kk * jnp.exp(b_end - bh)).astype(BF16)
        st_scr[h] = st * jnp.exp(b_end) + _dot_tn(v, ke)


def _hgrn_kernel(zqf_ref, zff_ref, zvf_ref, zqb_ref, zfb_ref, zvb_ref, lb_ref, s0_ref,
                 ofw_ref, obw_ref, sfin_ref, st_scr, b_scr, k_scr, *, n_chunk):
    c = pl.program_id(1)

    @pl.when(c == 0)
    def _():
        st_scr[...] = s0_ref[...]

    _hgrn_direction(zqf_ref, zff_ref, zvf_ref, lb_ref[0:1, :], st_scr.at[0], b_scr.at[0],
                    k_scr.at[0], ofw_ref, reverse=False)
    _hgrn_direction(zqb_ref, zfb_ref, zvb_ref, lb_ref[1:2, :], st_scr.at[1], b_scr.at[1],
                    k_scr.at[1], obw_ref, reverse=True)

    @pl.when(c == n_chunk - 1)
    def _():
        sfin_ref[...] = st_scr[...]


def _hgrn(z, lb, s0t, n_seq, seq_len):
    n_chunk = seq_len // CH
    t = n_seq * seq_len

    def fwd(col):
        return pl.BlockSpec((CH, COL_BLOCK), lambda s, c: (s * n_chunk + c, col))

    def bwd(col):
        return pl.BlockSpec((CH, COL_BLOCK), lambda s, c: (s * n_chunk + n_chunk - 1 - c, col))

    state_spec = pl.BlockSpec((None, 2, H_B, DV, DK), lambda s, c: (s, 0, 0, 0, 0))
    return pl.pallas_call(
        functools.partial(_hgrn_kernel, n_chunk=n_chunk),
        grid=(n_seq, n_chunk),
        in_specs=[fwd(2), fwd(3), fwd(5), bwd(2), bwd(4), bwd(5),
                  pl.BlockSpec((2, QK_WIDTH), lambda s, c: (0, 0)), state_spec],
        out_specs=[fwd(0), bwd(0), state_spec],
        out_shape=[_sds((t, H_B * DV), F32), _sds((t, H_B * DV), F32),
                   _sds((n_seq, 2, H_B, DV, DK), F32)],
        scratch_shapes=[pltpu.VMEM((2, H_B, DV, DK), F32),
                        pltpu.VMEM((2, CH, QK_WIDTH), F32),
                        pltpu.VMEM((2, CH, QK_WIDTH), F32)],
        compiler_params=_params("arbitrary", "arbitrary"),
        name="hgrn",
    )(z, z, z, z, z, z, lb, s0t)


def _merge_kernel(*refs, add_pos):
    refs = list(refs)
    x_ref = refs.pop(0)
    pos_ref = refs.pop(0) if add_pos else None
    (zu_ref, zv_ref, zg_ref, za0_ref, za1_ref, zb0_ref, zb1_ref, ofw_ref, obw_ref, mod_ref,
     sgug_ref, ws_ref, bst_ref, hng_ref, wpa_ref, wpb_ref, wout_ref, n2g_ref,
     x1_ref, hn2_ref) = refs

    u = jax.nn.gelu(zu_ref[...])
    v = (_rms(jax.nn.gelu(zv_ref[...])) * sgug_ref[...]).astype(BF16)
    row_parts = []
    for ci in range(TB_MERGE // CHUNK_MLP):
        col_parts = []
        for g in range(A_GROUPS):
            vg = v[ci * CHUNK_MLP:(ci + 1) * CHUNK_MLP, g * A_GROUP_DIM:(g + 1) * A_GROUP_DIM]
            col_parts.append(_dot(ws_ref[g], vg) + bst_ref[:, g:g + 1])
        row_parts.append(jnp.concatenate(col_parts, axis=1))
    y_a = u * jnp.concatenate(row_parts, axis=0)

    o = ofw_ref[...] + obw_ref[...]
    on = jnp.concatenate([_rms(o[:, h * DV:(h + 1) * DV]) for h in range(H_B)], axis=1)
    y_b = on * hng_ref[...] * jax.nn.silu(zg_ref[...])

    pa = _dot(y_a.astype(BF16), wpa_ref[...])
    pb = _dot(y_b.astype(BF16), wpb_ref[...])
    za = jnp.concatenate([za0_ref[...], za1_ref[...]], axis=1)
    zb = jnp.concatenate([zb0_ref[...], zb1_ref[...]], axis=1)
    mix_in = jax.nn.sigmoid(za) * pa + jax.nn.sigmoid(zb) * pb
    mix = _dot(mix_in.astype(BF16), wout_ref[...])

    x = x_ref[...]
    if add_pos:
        x = x + pos_ref[...]
    x1 = x + mod_ref[2:3, :] * mix
    x1_ref[...] = x1
    hn2 = _rms(x1) * n2g_ref[...] * (1.0 + mod_ref[4:5, :]) + mod_ref[3:4, :]
    hn2_ref[...] = hn2.astype(BF16)


def _merge(x2d, pos, z, o_fw, o_bw, mod, mod_row, seq_len, sgu_g, ws_bf, bs_t, hgrn_g,
           wpa_bf, wpb_bf, wout_bf, norm2_g):
    t = x2d.shape[0]
    tb = TB_MERGE
    bps = seq_len // tb
    add_pos = pos is not None

    def zcol(k):
        return pl.BlockSpec((tb, COL_BLOCK), lambda i: (i, k))

    def full(shape):
        return pl.BlockSpec(shape, lambda i: (0,) * len(shape))

    in_specs = [pl.BlockSpec((tb, D_MODEL), lambda i: (i, 0))]
    args = [x2d]
    if add_pos:
        in_specs.append(pl.BlockSpec((tb, D_MODEL), lambda i: (i % bps, 0)))
        args.append(pos)
    in_specs += [zcol(0), zcol(1), zcol(6), zcol(7), zcol(8), zcol(9), zcol(10),
                 pl.BlockSpec((tb, A_WIDTH), lambda i: (i, 0)),
                 pl.BlockSpec((tb, A_WIDTH), lambda i: (i, 0)),
                 pl.BlockSpec((None, N_MOD, D_MODEL), lambda i: (mod_row(i * tb), 0, 0)),
                 full((1, A_WIDTH)), full((A_GROUPS, CHUNK_MLP, CHUNK_MLP)),
                 full((CHUNK_MLP, A_GROUPS)), full((1, H_B * DV)),
                 full((A_WIDTH, D_MODEL)), full((H_B * DV, D_MODEL)),
                 full((D_MODEL, D_MODEL)), full((1, D_MODEL))]
    args += [z] * 7 + [o_fw, o_bw, mod, sgu_g, ws_bf, bs_t, hgrn_g, wpa_bf, wpb_bf, wout_bf, norm2_g]
    return pl.pallas_call(
        functools.partial(_merge_kernel, add_pos=add_pos),
        grid=(t // tb,),
        in_specs=in_specs,
        out_specs=[pl.BlockSpec((tb, D_MODEL), lambda i: (i, 0)),
                   pl.BlockSpec((tb, D_MODEL), lambda i: (i, 0))],
        out_shape=[_sds((t, D_MODEL), F32), _sds((t, D_MODEL), BF16)],
        compiler_params=_params("arbitrary"),
        name="merge",
    )(*args)


def _sort_network(n):
    def merge(lo, hi, r):
        step = r * 2
        if step < hi - lo:
            yield from merge(lo, hi, step)
            yield from merge(lo + r, hi, step)
            yield from [(i, i + r) for i in range(lo + r, hi - r, step)]
        else:
            yield (lo, lo + r)

    def sort(lo, hi):
        if hi - lo >= 1:
            mid = lo + (hi - lo) // 2
            yield from sort(lo, mid)
            yield from sort(mid + 1, hi)
            yield from merge(lo, hi, 1)

    return tuple(sort(0, n - 1))


def _col_max(x):
    return jnp.max(x, axis=0, keepdims=True)


def _col_min(x):
    return jnp.min(x, axis=0, keepdims=True)


def _topk_kernel(*refs, cast_table):
    if cast_table:
        (hn_ref, wq_ref, keys_ref, tab_ref, idx_ref, g_ref, tab_out_ref,
         v_scr, i_scr, tv_scr, fp_scr, idxt_scr, gt_scr) = refs
        tab_out_ref[...] = tab_ref[...].astype(BF16)
    else:
        (hn_ref, wq_ref, keys_ref, idx_ref, g_ref,
         v_scr, i_scr, tv_scr, fp_scr, idxt_scr, gt_scr) = refs
    tb = hn_ref.shape[0]
    k = PEER_TOPK
    q = _dot(hn_ref[...], wq_ref[...]).astype(BF16)
    sub = lax.broadcasted_iota(I32, (SUBLANES, tb), 0).astype(F32)
    n_tiles = N_KEYS // SUBLANES
    neg = -jnp.inf

    for h in range(PEER_HEADS):
        for p in range(2):
            hp = 2 * h + p
            s = _dot_nt(keys_ref[hp], q[:, hp * PEER_QHALF:(hp + 1) * PEER_QHALF])
            vals = [s[SUBLANES * v:SUBLANES * (v + 1), :] for v in range(n_tiles)]
            ids = [sub + float(SUBLANES * v) for v in range(n_tiles)]
            for i, j in _sort_network(n_tiles):
                a, b, ia, ib = vals[i], vals[j], ids[i], ids[j]
                swap = (b > a) | ((b == a) & (ib < ia))
                vals[i], vals[j] = jnp.where(swap, b, a), jnp.where(swap, a, b)
                ids[i], ids[j] = jnp.where(swap, ib, ia), jnp.where(swap, ia, ib)
            for r in range(k):
                m = _col_max(vals[0])
                first = _col_min(jnp.where(vals[0] == m, ids[0], float(N_KEYS)))
                sel = ids[0] == first
                v_scr[p, r:r + 1, :] = m
                i_scr[p, r:r + 1, :] = first
                left = k - 1 - r
                for d in range(left):
                    vals[d] = jnp.where(sel, vals[d + 1], vals[d])
                    ids[d] = jnp.where(sel, ids[d + 1], ids[d])

        lists = []
        for r in range(k):
            val = v_scr[0, r:r + 1, :] + v_scr[1, 0:SUBLANES, :]
            n_valid = k // (r + 1)
            lists.append(val if n_valid >= SUBLANES else jnp.where(sub < float(n_valid), val, neg))
        extra = v_scr[0, 0:1, :] + v_scr[1, SUBLANES:k, :]
        flat = sub
        flat_extra = sub + float(SUBLANES)
        for j in range(k):
            m = _col_max(jnp.maximum(lists[0], extra))
            first = _col_min(jnp.minimum(jnp.where(lists[0] == m, flat, float(k * k)),
                                         jnp.where(extra == m, flat_extra, float(k * k))))
            tv_scr[j:j + 1, :] = m
            fp_scr[j:j + 1, :] = first
            left = k - 1 - j
            if left:
                sel = flat == first
                for d in range(left):
                    lists[d] = jnp.where(sel, lists[d + 1], lists[d])
                flat = jnp.where(sel, flat + float(k), flat)
                extra = jnp.where(flat_extra == first, neg, extra)

        fp = fp_scr[...]
        pr = jnp.floor(fp * (1.0 / k))
        pc = fp - pr * float(k)
        e1 = jnp.zeros_like(fp)
        e2 = jnp.zeros_like(fp)
        for r in range(k):
            e1 = jnp.where(pr == float(r), i_scr[0, r:r + 1, :], e1)
            e2 = jnp.where(pc == float(r), i_scr[1, r:r + 1, :], e2)
        idxt_scr[h * k:(h + 1) * k, :] = (e1 * float(N_KEYS) + e2).astype(I32)
        tv = tv_scr[...]
        ex = jnp.exp(tv - tv[0:1, :])
        gt_scr[h * k:(h + 1) * k, :] = ex / jnp.sum(ex, axis=0, keepdims=True)

    idx_ref[...] = idxt_scr[...].T
    g_ref[...] = gt_scr[...].T


def _topk(hn2, wq_bf, keys_bf, table=None):
    t = hn2.shape[0]
    tb = TB_TOPK
    hk = PEER_HEADS * PEER_TOPK
    steps = t // tb
    in_specs = [pl.BlockSpec((tb, D_MODEL), lambda i: (i, 0)),
                pl.BlockSpec((D_MODEL, PEER_Q), lambda i: (0, 0)),
                pl.BlockSpec((2 * PEER_HEADS, N_KEYS, PEER_QHALF), lambda i: (0, 0, 0))]
    out_specs = [pl.BlockSpec((tb, hk), lambda i: (i, 0)), pl.BlockSpec((tb, hk), lambda i: (i, 0))]
    out_shape = [_sds((t, hk), I32), _sds((t, hk), F32)]
    args = [hn2, wq_bf, keys_bf]
    if table is not None:
        rows = table.shape[0] // steps
        assert rows * steps == table.shape[0] and rows % 16 == 0
        in_specs.append(pl.BlockSpec((rows, table.shape[1]), lambda i: (i, 0)))
        out_specs.append(pl.BlockSpec((rows, table.shape[1]), lambda i: (i, 0)))
        out_shape.append(_sds(table.shape, BF16))
        args.append(table)
    return pl.pallas_call(
        functools.partial(_topk_kernel, cast_table=table is not None),
        grid=(steps,),
        in_specs=in_specs,
        out_specs=out_specs,
        out_shape=out_shape,
        scratch_shapes=[pltpu.VMEM((2, PEER_TOPK, tb), F32),
                        pltpu.VMEM((2, PEER_TOPK, tb), F32),
                        pltpu.VMEM((PEER_TOPK, tb), F32),
                        pltpu.VMEM((PEER_TOPK, tb), F32),
                        pltpu.VMEM((hk, tb), I32),
                        pltpu.VMEM((hk, tb), F32)],
        compiler_params=_params("arbitrary"),
        name="peer_topk",
    )(*args)


def _stream_copy(src_hbm, buf, sem, step, block):
    slot = lax.rem(step, STREAM_SLOTS)
    return pltpu.make_async_copy(src_hbm.at[pl.ds(block * E_BLOCK, E_BLOCK)], buf.at[slot], sem.at[slot])


def _stream_prefetch(copy_for_step, step, total):
    @pl.when(step == 0)
    def _():
        for k in range(STREAM_AHEAD):
            copy_for_step(k).start()

    @pl.when(step + STREAM_AHEAD < total)
    def _():
        copy_for_step(step + STREAM_AHEAD).start()


def _peer_up_kernel(hn_ref, u_hbm, idx_ref, p_ref, s_even, s_odd, ubuf, usem):
    s = pl.program_id(0)
    total = pl.num_programs(0)
    n_blocks = N_EXPERTS // E_BLOCK
    per = E_BLOCK // N_KEYS

    def u_copy(step):
        return _stream_copy(u_hbm, ubuf, usem, step, lax.rem(jnp.minimum(step, total - 2), n_blocks))

    _stream_prefetch(u_copy, s, total)

    @pl.when(s == 0)
    def _():
        p_ref[...] = jnp.zeros_like(p_ref)
        s_odd[...] = jnp.zeros_like(s_odd)

    u_copy(s).wait()
    slot = lax.rem(s, STREAM_SLOTS)
    prev_step = jnp.maximum(s - 1, 0)
    row0 = pl.multiple_of(lax.div(prev_step, n_blocks) * TB_UP, TB_UP)
    first_row = jnp.where(s == 0, -per, lax.rem(prev_step, n_blocks) * per)

    def step(cur, prev):
        cur[...] = _dot_nt(hn_ref[...], ubuf[slot].astype(BF16))
        for c in range(TB_UP // SUBLANES):
            rows = pl.ds(row0 + c * SUBLANES, SUBLANES)
            idx = idx_ref[rows, :]
            i1 = idx >> KEY_BITS
            i2 = idx & (N_KEYS - 1)
            acc = p_ref[rows, :]
            for k in range(per):
                picked = jnp.take_along_axis(
                    prev[c * SUBLANES:(c + 1) * SUBLANES, k * N_KEYS:(k + 1) * N_KEYS], i2, axis=1)
                acc = jnp.where(i1 == first_row + k, picked, acc)
            p_ref[rows, :] = acc

    @pl.when(s % 2 == 0)
    def _():
        step(s_even, s_odd)

    @pl.when(s % 2 == 1)
    def _():
        step(s_odd, s_even)


def _peer_up(hn2, u, idx):
    t = hn2.shape[0]
    hk = PEER_HEADS * PEER_TOPK
    n_blocks = N_EXPERTS // E_BLOCK
    nb = t // TB_UP
    whole = pl.BlockSpec((t, hk), lambda s: (0, 0))
    return pl.pallas_call(
        _peer_up_kernel,
        grid=(nb * n_blocks + 1,),
        in_specs=[pl.BlockSpec((TB_UP, D_MODEL), lambda s: (jnp.minimum(s // n_blocks, nb - 1), 0)),
                  pl.BlockSpec(memory_space=pl.ANY),
                  whole],
        out_specs=whole,
        out_shape=_sds((t, hk), F32),
        scratch_shapes=[pltpu.VMEM((TB_UP, E_BLOCK), F32), pltpu.VMEM((TB_UP, E_BLOCK), F32),
                        pltpu.VMEM((STREAM_SLOTS, E_BLOCK, D_MODEL), u.dtype),
                        pltpu.SemaphoreType.DMA((STREAM_SLOTS,))],
        compiler_params=_params("arbitrary"),
        name="peer_up",
    )(hn2, u, idx)


def _peer_down_kernel(p_ref, g_ref, idx_ref, v_hbm, x1_ref, mod_ref, fg_ref, y_ref, a_scr, abuf, vbuf, vsem):
    j = pl.program_id(1)
    nj = pl.num_programs(1)
    step_id = pl.program_id(0) * nj + j

    def v_copy(s):
        return _stream_copy(v_hbm, vbuf, vsem, s, lax.rem(s, nj))

    _stream_prefetch(v_copy, step_id, pl.num_programs(0) * nj)
    v_copy(step_id).wait()

    @pl.when(j == 0)
    def _():
        y_ref[...] = jnp.zeros_like(y_ref)
        a_scr[...] = jax.nn.gelu(p_ref[...]) * g_ref[...]
        key_ids = lax.broadcasted_iota(I32, (N_KEYS, PEER_HEADS * PEER_TOPK), 0)

        def body(tt, carry):
            for u in range(SCATTER_UNROLL):
                t = tt * SCATTER_UNROLL + u
                idx = idx_ref[pl.ds(t, 1), :]
                pm = jnp.where((idx >> KEY_BITS) == key_ids, a_scr[pl.ds(t, 1), :], 0.0).astype(BF16)
                qm = jnp.where((idx & (N_KEYS - 1)) == key_ids, 1.0, 0.0).astype(BF16)
                abuf[pl.ds(pl.multiple_of(t * A_STRIDE, SUBLANES), N_KEYS), :] = _dot_nt(pm, qm)
            return carry

        lax.fori_loop(0, TB_DOWN // SCATTER_UNROLL, body, 0)

    per = E_BLOCK // N_KEYS
    cols = [abuf[pl.ds(j * per + k, TB_DOWN, stride=A_STRIDE), :] for k in range(per)]
    a_blk = jnp.concatenate(cols, axis=1).astype(BF16)
    y_ref[...] += _dot(a_blk, vbuf[lax.rem(step_id, STREAM_SLOTS)])

    @pl.when(j == nj - 1)
    def _():
        x2 = x1_ref[...] + mod_ref[5:6, :] * y_ref[...]
        y_ref[...] = _rms(x2) * fg_ref[...]


def _peer_down(p, g, idx, v_bf, x1, mod, mod_row, final_g):
    t = p.shape[0]
    hk = PEER_HEADS * PEER_TOPK
    tok = pl.BlockSpec((TB_DOWN, hk), lambda i, j: (i, 0))
    wide = pl.BlockSpec((TB_DOWN, D_MODEL), lambda i, j: (i, 0))
    return pl.pallas_call(
        _peer_down_kernel,
        grid=(t // TB_DOWN, N_EXPERTS // E_BLOCK),
        in_specs=[tok, tok, tok, pl.BlockSpec(memory_space=pl.ANY), wide,
                  pl.BlockSpec((None, N_MOD, D_MODEL), lambda i, j: (mod_row(i * TB_DOWN), 0, 0)),
                  pl.BlockSpec((1, D_MODEL), lambda i, j: (0, 0))],
        out_specs=wide,
        out_shape=_sds((t, D_MODEL), F32),
        scratch_shapes=[pltpu.VMEM((TB_DOWN, hk), F32),
                        pltpu.VMEM((TB_DOWN * A_STRIDE, N_KEYS), F32),
                        pltpu.VMEM((STREAM_SLOTS, E_BLOCK, D_MODEL), v_bf.dtype),
                        pltpu.SemaphoreType.DMA((STREAM_SLOTS,))],
        compiler_params=_params("arbitrary", "arbitrary"),
        name="peer_down",
    )(p, g, idx, v_bf, x1, mod, final_g)


def _grid_pos_embed(n_tokens):
    rows = n_tokens // GRID_W
    r = np.repeat(np.arange(rows, dtype=np.float64), GRID_W)
    col = np.tile(np.arange(GRID_W, dtype=np.float64), rows)
    quarter = D_MODEL // 4
    omega = 1.0 / (POS_THETA ** (np.arange(quarter, dtype=np.float64) / quarter))
    ar = r[:, None] * omega
    ac = col[:, None] * omega
    table = np.concatenate([np.sin(ar), np.cos(ar), np.sin(ac), np.cos(ac)], axis=-1)
    return jnp.asarray(table.astype(np.float32))


def _trunk(x, pos, mod, mod_base, per_seq, s0t, lb, w, v_bf):
    n_seq, seq_len, _ = x.shape
    x2d = x.reshape(n_seq * seq_len, D_MODEL)

    def mod_row(tok):
        return mod_base + (tok // seq_len if per_seq else 0)

    z = _inproj(x2d, pos, mod, mod_row, w["norm1_g"], w["w_in"], seq_len)
    o_fw, o_bw, st_fin = _hgrn(z, lb, s0t, n_seq, seq_len)
    x1, hn2 = _merge(x2d, pos, z, o_fw, o_bw, mod, mod_row, seq_len, w["sgu_norm_g"], w["w_spatial"],
                     w["b_spatial_t"], w["hgrn_norm_g"], w["w_proj_a"], w["w_proj_b"], w["w_out"],
                     w["norm2_g"])
    if v_bf is None:
        idx, gate, v_bf = _topk(hn2, w["peer_w_q"], w["peer_sub_keys"], table=w["peer_v"])
    else:
        idx, gate = _topk(hn2, w["peer_w_q"], w["peer_sub_keys"])
    p = _peer_up(hn2, w["peer_u"], idx)
    y = _peer_down(p, gate, idx, v_bf, x1, mod, mod_row, w["final_norm_g"])
    return y.reshape(x.shape), jnp.swapaxes(st_fin, -1, -2), v_bf


def kernel(x_prompt, x_sample, state_hgrn, c, c_ctx, w_ada, b_ada, norm1_g, w_in, sgu_norm_g,
           w_spatial, b_spatial, hgrn_lb, hgrn_norm_g, w_proj_a, w_proj_b, w_out, norm2_g,
           peer_w_q, peer_sub_keys, peer_u, peer_v, final_norm_g):
    assert w_in.shape[0] == 1, "single-layer stack"
    n_p, l_p, _ = x_prompt.shape
    n_s, l_s, _ = x_sample.shape
    assert n_s + 1 <= SUBLANES
    lb = jnp.cumsum(jax.nn.softmax(hgrn_lb.astype(F32), axis=0), axis=0)[0]

    cond = jnp.zeros((SUBLANES, D_MODEL), F32).at[0].set(c_ctx).at[1:1 + n_s].set(c)
    mod, w_in_bf = _adaln(cond, w_ada[0], b_ada[0][None], w_in[0])
    mod = mod.reshape(SUBLANES, N_MOD, D_MODEL)

    w = dict(
        norm1_g=norm1_g[0][None], w_in=w_in_bf, sgu_norm_g=sgu_norm_g[0][None],
        w_spatial=w_spatial[0].astype(BF16), b_spatial_t=b_spatial[0].T,
        hgrn_norm_g=hgrn_norm_g[0].reshape(1, H_B * DV),
        w_proj_a=w_proj_a[0].astype(BF16), w_proj_b=w_proj_b[0].astype(BF16),
        w_out=w_out[0].astype(BF16), norm2_g=norm2_g[0][None],
        peer_w_q=peer_w_q[0].astype(BF16),
        peer_sub_keys=peer_sub_keys[0].reshape(2 * PEER_HEADS, N_KEYS, PEER_QHALF).astype(BF16),
        peer_u=peer_u[0], peer_v=peer_v[0],
        final_norm_g=final_norm_g[None],
    )

    s0_ctx = jnp.zeros((n_p, 2, H_B, DV, DK), F32)
    s0_lat = jnp.swapaxes(state_hgrn[:, 0].astype(F32), -1, -2)
    pos = _grid_pos_embed(l_s)

    y_prompt, s_ctx, v_bf = _trunk(x_prompt, None, mod, 0, False, s0_ctx, lb, w, None)
    y_sample, _, _ = _trunk(x_sample, pos, mod, 1, True, s0_lat, lb, w, v_bf)
    return (y_prompt, y_sample, s_ctx[:, None])
```

```python
import functools

import jax
import jax.numpy as jnp
import numpy as np
from jax import lax
from jax.experimental import pallas as pl
from jax.experimental.pallas import tpu as pltpu

F32 = jnp.float32
BF16 = jnp.bfloat16
I32 = jnp.int32

D_MODEL = 1024
N_MOD = 6
A_WIDTH = 512
A_GROUPS = 4
A_GROUP_DIM = A_WIDTH // A_GROUPS
CHUNK_MLP = 128
H_B = 4
DK = 128
DV = 128
QK_WIDTH = H_B * DK
IN_WIDTH = 5632
COL_BLOCK = 512
PEER_HEADS = 8
N_KEYS = 128
KEY_BITS = 7
PEER_TOPK = 16
PEER_QHALF = 128
PEER_Q = 2 * PEER_HEADS * PEER_QHALF
N_EXPERTS = N_KEYS * N_KEYS
GRID_W = 64
POS_THETA = 10000.0
EPS = 1e-6

SUBLANES = 8
LANES = 128
VMEM_LIMIT_BYTES = 56 * 1024 * 1024

TB_IN = 512
CH = 128
TB_MERGE = 512
TB_TOPK = 256
TB_UP = 1024
TB_DOWN = 512
E_BLOCK = 1024
A_STRIDE = 136
SCATTER_UNROLL = 128
STREAM_AHEAD = 2
STREAM_SLOTS = STREAM_AHEAD + 1


def _sds(shape, dtype):
    return jax.ShapeDtypeStruct(shape, dtype)


def _params(*sem):
    return pltpu.CompilerParams(dimension_semantics=sem, vmem_limit_bytes=VMEM_LIMIT_BYTES)


def _dot(a, b):
    return jnp.dot(a, b, preferred_element_type=F32)


def _dot_nt(a, b):
    return lax.dot_general(a, b, (((1,), (1,)), ((), ())), preferred_element_type=F32)


def _dot_tn(a, b):
    return lax.dot_general(a, b, (((0,), (0,)), ((), ())), preferred_element_type=F32)


def _rms(x):
    return x * lax.rsqrt(jnp.mean(x * x, axis=-1, keepdims=True) + EPS)


def _adaln_kernel(c_ref, w_ref, b_ref, tab_ref, o_ref, tab_out_ref):
    a = jax.nn.silu(c_ref[...])
    o_ref[...] = jnp.dot(a, w_ref[...], preferred_element_type=F32,
                         precision=lax.Precision.HIGHEST) + b_ref[...]
    tab_out_ref[...] = tab_ref[...].astype(BF16)


def _adaln(cond8, w_ada, b_ada, table):
    n = w_ada.shape[1]
    tn = 1536
    steps = n // tn
    rows = table.shape[0] // steps
    assert rows * steps == table.shape[0] and rows % 16 == 0
    tab_spec = pl.BlockSpec((rows, table.shape[1]), lambda j: (j, 0))
    return pl.pallas_call(
        _adaln_kernel,
        grid=(steps,),
        in_specs=[pl.BlockSpec((SUBLANES, D_MODEL), lambda j: (0, 0)),
                  pl.BlockSpec((D_MODEL, tn), lambda j: (0, j)),
                  pl.BlockSpec((1, tn), lambda j: (0, j)),
                  tab_spec],
        out_specs=[pl.BlockSpec((SUBLANES, tn), lambda j: (0, j)), tab_spec],
        out_shape=[_sds((SUBLANES, n), F32), _sds(table.shape, BF16)],
        compiler_params=_params("arbitrary"),
        name="adaln",
    )(cond8, w_ada, b_ada, table)


def _inproj_kernel(*refs, add_pos):
    if add_pos:
        x_ref, pos_ref, mod_ref, g_ref, w_ref, z_ref = refs
    else:
        x_ref, mod_ref, g_ref, w_ref, z_ref = refs
    x = x_ref[...]
    if add_pos:
        x = x + pos_ref[...]
    h = _rms(x) * g_ref[...] * (1.0 + mod_ref[1:2, :]) + mod_ref[0:1, :]
    z_ref[...] = _dot(h.astype(BF16), w_ref[...])


def _inproj(x2d, pos, mod, mod_row, norm_g, w_in_bf, seq_len):
    t = x2d.shape[0]
    bps = seq_len // TB_IN
    add_pos = pos is not None
    in_specs = [pl.BlockSpec((TB_IN, D_MODEL), lambda i: (i, 0))]
    args = [x2d]
    if add_pos:
        in_specs.append(pl.BlockSpec((TB_IN, D_MODEL), lambda i: (i % bps, 0)))
        args.append(pos)
    in_specs += [pl.BlockSpec((None, N_MOD, D_MODEL), lambda i: (mod_row(i * TB_IN), 0, 0)),
                 pl.BlockSpec((1, D_MODEL), lambda i: (0, 0)),
                 pl.BlockSpec((D_MODEL, IN_WIDTH), lambda i: (0, 0), pipeline_mode=pl.Buffered(1))]
    args += [mod, norm_g, w_in_bf]
    return pl.pallas_call(
        functools.partial(_inproj_kernel, add_pos=add_pos),
        grid=(t // TB_IN,),
        in_specs=in_specs,
        out_specs=pl.BlockSpec((TB_IN, IN_WIDTH), lambda i: (i, 0)),
        out_shape=_sds((t, IN_WIDTH), F32),
        compiler_params=_params("arbitrary"),
        name="inproj",
    )(*args)


def _split3(x):
    hi = x.astype(BF16)
    r1 = x - hi.astype(F32)
    mid = r1.astype(BF16)
    lo = (r1 - mid.astype(F32)).astype(BF16)
    return hi, mid, lo


def _hgrn_direction(zq_ref, zf_ref, zv_ref, lb, st_scr, b_scr, k_scr, o_ref, reverse):
    f = lb + (1.0 - lb) * jax.nn.sigmoid(zf_ref[...])
    k_scr[...] = 1.0 - f
    lf = jnp.log(f)
    row = lax.broadcasted_iota(I32, (CH, CH), 0)
    col = lax.broadcasted_iota(I32, (CH, CH), 1)
    tri = jnp.where((row <= col) if reverse else (row >= col), 1.0, 0.0).astype(BF16)
    hi, mid, lo = _split3(lf)
    b_scr[...] = _dot(tri, hi) + _dot(tri, mid) + _dot(tri, lo)

    sub = lax.broadcasted_iota(I32, (SUBLANES, DK), 0)
    zero_tile = jnp.zeros((SUBLANES, DK), F32)
    n_tiles = CH // SUBLANES
    end_row = 0 if reverse else CH - 1

    for h in range(H_B):
        sl = slice(h * DK, (h + 1) * DK)
        q = zq_ref[:, sl]
        kk = k_scr[:, sl]
        bh = b_scr[:, sl]

        def masked(sc, m):
            if 2 * m >= CH:
                return sc
            shift = (2 * m).bit_length() - 1 if m else 0
            return jnp.where((row >> shift) == (col >> shift), sc, 0.0)

        s_tot = masked(_dot_nt(q.astype(BF16), kk.astype(BF16)), 0)

        m = 1
        while m < CH:
            q_tiles, k_tiles = [], []
            for j in range(n_tiles):
                r0 = j * SUBLANES
                rows = slice(r0, r0 + SUBLANES)
                if m < SUBLANES:
                    upper = (sub & m) != 0
                    gives_q = ~upper if reverse else upper
                    if m == 1:
                        scale_q = 1.0 - k_scr[rows, sl]
                        scale_k = None
                    else:
                        seam = m if reverse else m - 1
                        if 2 * m == SUBLANES:
                            b_mid = b_scr[r0 + seam:r0 + seam + 1, sl]
                        else:
                            b_mid = jnp.where(sub < 2 * m, b_scr[r0 + seam:r0 + seam + 1, sl],
                                              b_scr[r0 + 2 * m + seam:r0 + 2 * m + seam + 1, sl])
                        d = b_scr[rows, sl] - b_mid
                        scale_q = scale_k = jnp.exp(jnp.minimum(d, -d))
                    qt = zq_ref[rows, sl] * scale_q
                    kt = k_scr[rows, sl] if scale_k is None else k_scr[rows, sl] * scale_k
                    q_tiles.append(jnp.where(gives_q, qt, 0.0))
                    k_tiles.append(jnp.where(gives_q, 0.0, kt))
                else:
                    blk = r0 // m
                    gives_q = (blk % 2 == 0) if reverse else (blk % 2 == 1)
                    if gives_q:
                        ref_row = blk * m + m if reverse else blk * m - 1
                        q_tiles.append(zq_ref[rows, sl] * jnp.exp(
                            b_scr[rows, sl] - b_scr[ref_row:ref_row + 1, sl]))
                        k_tiles.append(zero_tile)
                    else:
                        ref_row = blk * m if reverse else blk * m + m - 1
                        k_tiles.append(k_scr[rows, sl] * jnp.exp(
                            b_scr[ref_row:ref_row + 1, sl] - b_scr[rows, sl]))
                        q_tiles.append(zero_tile)
            qm = jnp.concatenate(q_tiles, axis=0).astype(BF16)
            km = jnp.concatenate(k_tiles, axis=0).astype(BF16)
            s_tot = s_tot + masked(_dot_nt(qm, km), m)
            m *= 2

        v = zv_ref[:, sl].astype(BF16)
        st = st_scr[h]
        o_ref[:, sl] = (_dot(s_tot.astype(BF16), v)
                        + _dot_nt((q * jnp.exp(bh)).astype(BF16), st.astype(BF16)))
        b_end = b_scr[end_row:end_row + 1, sl]
        ke = (kk * jnp.exp(b_end - bh)).astype(BF16)
        st_scr[h] = st * jnp.exp(b_end) + _dot_tn(v, ke)


def _hgrn_kernel(*refs, n_chunk, has_s0):
    refs = list(refs)
    zqf_ref, zff_ref, zvf_ref, zqb_ref, zfb_ref, zvb_ref, lb_ref = refs[:7]
    s0_ref = refs[7] if has_s0 else None
    ofw_ref, obw_ref, sfin_ref, st_scr, b_scr, k_scr = refs[-6:]
    c = pl.program_id(1)

    @pl.when(c == 0)
    def _():
        for d in range(2):
            for h in range(H_B):
                st_scr[d, h] = s0_ref[d, h].T if has_s0 else jnp.zeros((DV, DK), F32)

    _hgrn_direction(zqf_ref, zff_ref, zvf_ref, lb_ref[0:1, :], st_scr.at[0], b_scr.at[0],
                    k_scr.at[0], ofw_ref, reverse=False)
    _hgrn_direction(zqb_ref, zfb_ref, zvb_ref, lb_ref[1:2, :], st_scr.at[1], b_scr.at[1],
                    k_scr.at[1], obw_ref, reverse=True)

    @pl.when(c == n_chunk - 1)
    def _():
        for d in range(2):
            for h in range(H_B):
                sfin_ref[d, h] = st_scr[d, h].T


def _hgrn(z, lb, s0, n_seq, seq_len):
    n_chunk = seq_len // CH
    t = n_seq * seq_len

    def fwd(col):
        return pl.BlockSpec((CH, COL_BLOCK), lambda s, c: (s * n_chunk + c, col))

    def bwd(col):
        return pl.BlockSpec((CH, COL_BLOCK), lambda s, c: (s * n_chunk + n_chunk - 1 - c, col))

    state_spec = pl.BlockSpec((None, 2, H_B, DK, DV), lambda s, c: (s, 0, 0, 0, 0))
    in_specs = [fwd(2), fwd(3), fwd(5), bwd(2), bwd(4), bwd(5),
                pl.BlockSpec((2, QK_WIDTH), lambda s, c: (0, 0))]
    args = [z, z, z, z, z, z, lb]
    if s0 is not None:
        in_specs.append(state_spec)
        args.append(s0)
    return pl.pallas_call(
        functools.partial(_hgrn_kernel, n_chunk=n_chunk, has_s0=s0 is not None),
        grid=(n_seq, n_chunk),
        in_specs=in_specs,
        out_specs=[fwd(0), bwd(0), state_spec],
        out_shape=[_sds((t, H_B * DV), F32), _sds((t, H_B * DV), F32),
                   _sds((n_seq, 2, H_B, DK, DV), F32)],
        scratch_shapes=[pltpu.VMEM((2, H_B, DV, DK), F32),
                        pltpu.VMEM((2, CH, QK_WIDTH), F32),
                        pltpu.VMEM((2, CH, QK_WIDTH), F32)],
        compiler_params=_params("arbitrary", "arbitrary"),
        name="hgrn",
    )(*args)


def _merge_kernel(*refs, add_pos):
    refs = list(refs)
    x_ref = refs.pop(0)
    pos_ref = refs.pop(0) if add_pos else None
    (zu_ref, zv_ref, zg_ref, za0_ref, za1_ref, zb0_ref, zb1_ref, ofw_ref, obw_ref, mod_ref,
     sgug_ref, ws_ref, bst_ref, hng_ref, wpa_ref, wpb_ref, wout_ref, n2g_ref,
     x1_ref, hn2_ref) = refs

    u = jax.nn.gelu(zu_ref[...])
    v = (_rms(jax.nn.gelu(zv_ref[...])) * sgug_ref[...]).astype(BF16)
    row_parts = []
    for ci in range(TB_MERGE // CHUNK_MLP):
        col_parts = []
        for g in range(A_GROUPS):
            vg = v[ci * CHUNK_MLP:(ci + 1) * CHUNK_MLP, g * A_GROUP_DIM:(g + 1) * A_GROUP_DIM]
            col_parts.append(_dot(ws_ref[g], vg) + bst_ref[:, g:g + 1])
        row_parts.append(jnp.concatenate(col_parts, axis=1))
    y_a = u * jnp.concatenate(row_parts, axis=0)

    o = ofw_ref[...] + obw_ref[...]
    on = jnp.concatenate([_rms(o[:, h * DV:(h + 1) * DV]) for h in range(H_B)], axis=1)
    y_b = on * hng_ref[...] * jax.nn.silu(zg_ref[...])

    pa = _dot(y_a.astype(BF16), wpa_ref[...])
    pb = _dot(y_b.astype(BF16), wpb_ref[...])
    za = jnp.concatenate([za0_ref[...], za1_ref[...]], axis=1)
    zb = jnp.concatenate([zb0_ref[...], zb1_ref[...]], axis=1)
    mix_in = jax.nn.sigmoid(za) * pa + jax.nn.sigmoid(zb) * pb
    mix = _dot(mix_in.astype(BF16), wout_ref[...])

    x = x_ref[...]
    if add_pos:
        x = x + pos_ref[...]
    x1 = x + mod_ref[2:3, :] * mix
    x1_ref[...] = x1
    hn2 = _rms(x1) * n2g_ref[...] * (1.0 + mod_ref[4:5, :]) + mod_ref[3:4, :]
    hn2_ref[...] = hn2.astype(BF16)


def _merge(x2d, pos, z, o_fw, o_bw, mod, mod_row, seq_len, sgu_g, ws_bf, bs_t, hgrn_g,
           wpa_bf, wpb_bf, wout_bf, norm2_g):
    t = x2d.shape[0]
    tb = TB_MERGE
    bps = seq_len // tb
    add_pos = pos is not None

    def zcol(k):
        return pl.BlockSpec((tb, COL_BLOCK), lambda i: (i, k))

    def full(shape):
        return pl.BlockSpec(shape, lambda i: (0,) * len(shape))

    in_specs = [pl.BlockSpec((tb, D_MODEL), lambda i: (i, 0))]
    args = [x2d]
    if add_pos:
        in_specs.append(pl.BlockSpec((tb, D_MODEL), lambda i: (i % bps, 0)))
        args.append(pos)
    in_specs += [zcol(0), zcol(1), zcol(6), zcol(7), zcol(8), zcol(9), zcol(10),
                 pl.BlockSpec((tb, A_WIDTH), lambda i: (i, 0)),
                 pl.BlockSpec((tb, A_WIDTH), lambda i: (i, 0)),
                 pl.BlockSpec((None, N_MOD, D_MODEL), lambda i: (mod_row(i * tb), 0, 0)),
                 full((1, A_WIDTH)), full((A_GROUPS, CHUNK_MLP, CHUNK_MLP)),
                 full((CHUNK_MLP, A_GROUPS)), full((1, H_B * DV)),
                 full((A_WIDTH, D_MODEL)), full((H_B * DV, D_MODEL)),
                 full((D_MODEL, D_MODEL)), full((1, D_MODEL))]
    args += [z] * 7 + [o_fw, o_bw, mod, sgu_g, ws_bf, bs_t, hgrn_g, wpa_bf, wpb_bf, wout_bf, norm2_g]
    return pl.pallas_call(
        functools.partial(_merge_kernel, add_pos=add_pos),
        grid=(t // tb,),
        in_specs=in_specs,
        out_specs=[pl.BlockSpec((tb, D_MODEL), lambda i: (i, 0)),
                   pl.BlockSpec((tb, D_MODEL), lambda i: (i, 0))],
        out_shape=[_sds((t, D_MODEL), F32), _sds((t, D_MODEL), BF16)],
        compiler_params=_params("arbitrary"),
        name="merge",
    )(*args)


def _sort_network(n):
    def merge(lo, hi, r):
        step = r * 2
        if step < hi - lo:
            yield from merge(lo, hi, step)
            yield from merge(lo + r, hi, step)
            yield from [(i, i + r) for i in range(lo + r, hi - r, step)]
        else:
            yield (lo, lo + r)

    def sort(lo, hi):
        if hi - lo >= 1:
            mid = lo + (hi - lo) // 2
            yield from sort(lo, mid)
            yield from sort(mid + 1, hi)
            yield from merge(lo, hi, 1)

    return tuple(sort(0, n - 1))


def _col_max(x):
    return jnp.max(x, axis=0, keepdims=True)


def _col_min(x):
    return jnp.min(x, axis=0, keepdims=True)


def _topk_kernel(*refs, cast_table):
    if cast_table:
        (hn_ref, wq_ref, keys_ref, tab_ref, idx_ref, g_ref, tab_out_ref,
         v_scr, i_scr, tv_scr, fp_scr, idxt_scr, gt_scr) = refs
        tab_out_ref[...] = tab_ref[...].astype(BF16)
    else:
        (hn_ref, wq_ref, keys_ref, idx_ref, g_ref,
         v_scr, i_scr, tv_scr, fp_scr, idxt_scr, gt_scr) = refs
    tb = hn_ref.shape[0]
    k = PEER_TOPK
    q = _dot(hn_ref[...], wq_ref[...]).astype(BF16)
    sub = lax.broadcasted_iota(I32, (SUBLANES, tb), 0).astype(F32)
    n_tiles = N_KEYS // SUBLANES
    neg = -jnp.inf

    for h in range(PEER_HEADS):
        for p in range(2):
            hp = 2 * h + p
            s = _dot_nt(keys_ref[hp], q[:, hp * PEER_QHALF:(hp + 1) * PEER_QHALF])
            vals = [s[SUBLANES * v:SUBLANES * (v + 1), :] for v in range(n_tiles)]
            ids = [sub + float(SUBLANES * v) for v in range(n_tiles)]
            for i, j in _sort_network(n_tiles):
                a, b, ia, ib = vals[i], vals[j], ids[i], ids[j]
                swap = (b > a) | ((b == a) & (ib < ia))
                vals[i], vals[j] = jnp.where(swap, b, a), jnp.where(swap, a, b)
                ids[i], ids[j] = jnp.where(swap, ib, ia), jnp.where(swap, ia, ib)
            for r in range(k):
                m = _col_max(vals[0])
                first = _col_min(jnp.where(vals[0] == m, ids[0], float(N_KEYS)))
                sel = ids[0] == first
                v_scr[p, r:r + 1, :] = m
                i_scr[p, r:r + 1, :] = first
                left = k - 1 - r
                for d in range(left):
                    vals[d] = jnp.where(sel, vals[d + 1], vals[d])
                    ids[d] = jnp.where(sel, ids[d + 1], ids[d])

        lists = []
        for r in range(k):
            val = v_scr[0, r:r + 1, :] + v_scr[1, 0:SUBLANES, :]
            n_valid = k // (r + 1)
            lists.append(val if n_valid >= SUBLANES else jnp.where(sub < float(n_valid), val, neg))
        extra = v_scr[0, 0:1, :] + v_scr[1, SUBLANES:k, :]
        flat = sub
        flat_extra = sub + float(SUBLANES)
        for j in range(k):
            m = _col_max(jnp.maximum(lists[0], extra))
            first = _col_min(jnp.minimum(jnp.where(lists[0] == m, flat, float(k * k)),
                                         jnp.where(extra == m, flat_extra, float(k * k))))
            tv_scr[j:j + 1, :] = m
            fp_scr[j:j + 1, :] = first
            left = k - 1 - j
            if left:
                sel = flat == first
                for d in range(left):
                    lists[d] = jnp.where(sel, lists[d + 1], lists[d])
                flat = jnp.where(sel, flat + float(k), flat)
                extra = jnp.where(flat_extra == first, neg, extra)

        fp = fp_scr[...]
        pr = jnp.floor(fp * (1.0 / k))
        pc = fp - pr * float(k)
        e1 = jnp.zeros_like(fp)
        e2 = jnp.zeros_like(fp)
        for r in range(k):
            e1 = jnp.where(pr == float(r), i_scr[0, r:r + 1, :], e1)
            e2 = jnp.where(pc == float(r), i_scr[1, r:r + 1, :], e2)
        idxt_scr[h * k:(h + 1) * k, :] = (e1 * float(N_KEYS) + e2).astype(I32)
        tv = tv_scr[...]
        ex = jnp.exp(tv - tv[0:1, :])
        gt_scr[h * k:(h + 1) * k, :] = ex / jnp.sum(ex, axis=0, keepdims=True)

    idx_ref[...] = idxt_scr[...].T
    g_ref[...] = gt_scr[...].T


def _topk(hn2, wq_bf, keys_bf, table=None):
    t = hn2.shape[0]
    tb = TB_TOPK
    hk = PEER_HEADS * PEER_TOPK
    steps = t // tb
    in_specs = [pl.BlockSpec((tb, D_MODEL), lambda i: (i, 0)),
                pl.BlockSpec((D_MODEL, PEER_Q), lambda i: (0, 0)),
                pl.BlockSpec((2 * PEER_HEADS, N_KEYS, PEER_QHALF), lambda i: (0, 0, 0))]
    out_specs = [pl.BlockSpec((tb, hk), lambda i: (i, 0)), pl.BlockSpec((tb, hk), lambda i: (i, 0))]
    out_shape = [_sds((t, hk), I32), _sds((t, hk), F32)]
    args = [hn2, wq_bf, keys_bf]
    if table is not None:
        rows = table.shape[0] // steps
        assert rows * steps == table.shape[0] and rows % 16 == 0
        in_specs.append(pl.BlockSpec((rows, table.shape[1]), lambda i: (i, 0)))
        out_specs.append(pl.BlockSpec((rows, table.shape[1]), lambda i: (i, 0)))
        out_shape.append(_sds(table.shape, BF16))
        args.append(table)
    return pl.pallas_call(
        functools.partial(_topk_kernel, cast_table=table is not None),
        grid=(steps,),
        in_specs=in_specs,
        out_specs=out_specs,
        out_shape=out_shape,
        scratch_shapes=[pltpu.VMEM((2, PEER_TOPK, tb), F32),
                        pltpu.VMEM((2, PEER_TOPK, tb), F32),
                        pltpu.VMEM((PEER_TOPK, tb), F32),
                        pltpu.VMEM((PEER_TOPK, tb), F32),
                        pltpu.VMEM((hk, tb), I32),
                        pltpu.VMEM((hk, tb), F32)],
        compiler_params=_params("arbitrary"),
        name="peer_topk",
    )(*args)


def _stream_copy(src_hbm, buf, sem, step, block):
    slot = lax.rem(step, STREAM_SLOTS)
    return pltpu.make_async_copy(src_hbm.at[pl.ds(block * E_BLOCK, E_BLOCK)], buf.at[slot], sem.at[slot])


def _stream_prefetch(copy_for_step, step, total):
    @pl.when(step == 0)
    def _():
        for k in range(STREAM_AHEAD):
            copy_for_step(k).start()

    @pl.when(step + STREAM_AHEAD < total)
    def _():
        copy_for_step(step + STREAM_AHEAD).start()


def _peer_up_kernel(hn_ref, u_hbm, idx_ref, p_ref, s_even, s_odd, ubuf, usem):
    s = pl.program_id(0)
    total = pl.num_programs(0)
    n_blocks = N_EXPERTS // E_BLOCK
    per = E_BLOCK // N_KEYS

    def u_copy(step):
        return _stream_copy(u_hbm, ubuf, usem, step, lax.rem(jnp.minimum(step, total - 2), n_blocks))

    _stream_prefetch(u_copy, s, total)

    @pl.when(s == 0)
    def _():
        p_ref[...] = jnp.zeros_like(p_ref)
        s_odd[...] = jnp.zeros_like(s_odd)

    u_copy(s).wait()
    slot = lax.rem(s, STREAM_SLOTS)
    prev_step = jnp.maximum(s - 1, 0)
    row0 = pl.multiple_of(lax.div(prev_step, n_blocks) * TB_UP, TB_UP)
    first_row = jnp.where(s == 0, -per, lax.rem(prev_step, n_blocks) * per)

    def step(cur, prev):
        cur[...] = _dot_nt(hn_ref[...], ubuf[slot].astype(BF16))
        for c in range(TB_UP // SUBLANES):
            rows = pl.ds(row0 + c * SUBLANES, SUBLANES)
            idx = idx_ref[rows, :]
            i1 = idx >> KEY_BITS
            i2 = idx & (N_KEYS - 1)
            acc = p_ref[rows, :]
            for k in range(per):
                picked = jnp.take_along_axis(
                    prev[c * SUBLANES:(c + 1) * SUBLANES, k * N_KEYS:(k + 1) * N_KEYS], i2, axis=1)
                acc = jnp.where(i1 == first_row + k, picked, acc)
            p_ref[rows, :] = acc

    @pl.when(s % 2 == 0)
    def _():
        step(s_even, s_odd)

    @pl.when(s % 2 == 1)
    def _():
        step(s_odd, s_even)


def _peer_up(hn2, u, idx):
    t = hn2.shape[0]
    hk = PEER_HEADS * PEER_TOPK
    n_blocks = N_EXPERTS // E_BLOCK
    nb = t // TB_UP
    whole = pl.BlockSpec((t, hk), lambda s: (0, 0))
    return pl.pallas_call(
        _peer_up_kernel,
        grid=(nb * n_blocks + 1,),
        in_specs=[pl.BlockSpec((TB_UP, D_MODEL), lambda s: (jnp.minimum(s // n_blocks, nb - 1), 0)),
                  pl.BlockSpec(memory_space=pl.ANY),
                  whole],
        out_specs=whole,
        out_shape=_sds((t, hk), F32),
        scratch_shapes=[pltpu.VMEM((TB_UP, E_BLOCK), F32), pltpu.VMEM((TB_UP, E_BLOCK), F32),
                        pltpu.VMEM((STREAM_SLOTS, E_BLOCK, D_MODEL), u.dtype),
                        pltpu.SemaphoreType.DMA((STREAM_SLOTS,))],
        compiler_params=_params("arbitrary"),
        name="peer_up",
    )(hn2, u, idx)


def _peer_down_kernel(p_ref, g_ref, idx_ref, v_hbm, x1_ref, mod_ref, fg_ref, y_ref, a_scr, abuf, vbuf, vsem):
    j = pl.program_id(1)
    nj = pl.num_programs(1)
    step_id = pl.program_id(0) * nj + j

    def v_copy(s):
        return _stream_copy(v_hbm, vbuf, vsem, s, lax.rem(s, nj))

    _stream_prefetch(v_copy, step_id, pl.num_programs(0) * nj)
    v_copy(step_id).wait()

    @pl.when(j == 0)
    def _():
        y_ref[...] = jnp.zeros_like(y_ref)
        a_scr[...] = jax.nn.gelu(p_ref[...]) * g_ref[...]
        key_ids = lax.broadcasted_iota(I32, (N_KEYS, PEER_HEADS * PEER_TOPK), 0)

        def body(tt, carry):
            for u in range(SCATTER_UNROLL):
                t = tt * SCATTER_UNROLL + u
                idx = idx_ref[pl.ds(t, 1), :]
                pm = jnp.where((idx >> KEY_BITS) == key_ids, a_scr[pl.ds(t, 1), :], 0.0).astype(BF16)
                qm = jnp.where((idx & (N_KEYS - 1)) == key_ids, 1.0, 0.0).astype(BF16)
                abuf[pl.ds(pl.multiple_of(t * A_STRIDE, SUBLANES), N_KEYS), :] = _dot_nt(pm, qm)
            return carry

        lax.fori_loop(0, TB_DOWN // SCATTER_UNROLL, body, 0)

    per = E_BLOCK // N_KEYS
    cols = [abuf[pl.ds(j * per + k, TB_DOWN, stride=A_STRIDE), :] for k in range(per)]
    a_blk = jnp.concatenate(cols, axis=1).astype(BF16)
    y_ref[...] += _dot(a_blk, vbuf[lax.rem(step_id, STREAM_SLOTS)])

    @pl.when(j == nj - 1)
    def _():
        x2 = x1_ref[...] + mod_ref[5:6, :] * y_ref[...]
        y_ref[...] = _rms(x2) * fg_ref[...]


def _peer_down(p, g, idx, v_bf, x1, mod, mod_row, final_g):
    t = p.shape[0]
    hk = PEER_HEADS * PEER_TOPK
    tok = pl.BlockSpec((TB_DOWN, hk), lambda i, j: (i, 0))
    wide = pl.BlockSpec((TB_DOWN, D_MODEL), lambda i, j: (i, 0))
    return pl.pallas_call(
        _peer_down_kernel,
        grid=(t // TB_DOWN, N_EXPERTS // E_BLOCK),
        in_specs=[tok, tok, tok, pl.BlockSpec(memory_space=pl.ANY), wide,
                  pl.BlockSpec((None, N_MOD, D_MODEL), lambda i, j: (mod_row(i * TB_DOWN), 0, 0)),
                  pl.BlockSpec((1, D_MODEL), lambda i, j: (0, 0))],
        out_specs=wide,
        out_shape=_sds((t, D_MODEL), F32),
        scratch_shapes=[pltpu.VMEM((TB_DOWN, hk), F32),
                        pltpu.VMEM((TB_DOWN * A_STRIDE, N_KEYS), F32),
                        pltpu.VMEM((STREAM_SLOTS, E_BLOCK, D_MODEL), v_bf.dtype),
                        pltpu.SemaphoreType.DMA((STREAM_SLOTS,))],
        compiler_params=_params("arbitrary", "arbitrary"),
        name="peer_down",
    )(p, g, idx, v_bf, x1, mod, final_g)


def _grid_pos_embed(n_tokens):
    rows = n_tokens // GRID_W
    r = np.repeat(np.arange(rows, dtype=np.float64), GRID_W)
    col = np.tile(np.arange(GRID_W, dtype=np.float64), rows)
    quarter = D_MODEL // 4
    omega = 1.0 / (POS_THETA ** (np.arange(quarter, dtype=np.float64) / quarter))
    ar = r[:, None] * omega
    ac = col[:, None] * omega
    table = np.concatenate([np.sin(ar), np.cos(ar), np.sin(ac), np.cos(ac)], axis=-1)
    return jnp.asarray(table.astype(np.float32))


def _trunk(x, pos, mod, mod_base, per_seq, s0, lb, w, v_bf):
    n_seq, seq_len, _ = x.shape
    x2d = x.reshape(n_seq * seq_len, D_MODEL)

    def mod_row(tok):
        return mod_base + (tok // seq_len if per_seq else 0)

    z = _inproj(x2d, pos, mod, mod_row, w["norm1_g"], w["w_in"], seq_len)
    o_fw, o_bw, st_fin = _hgrn(z, lb, s0, n_seq, seq_len)
    x1, hn2 = _merge(x2d, pos, z, o_fw, o_bw, mod, mod_row, seq_len, w["sgu_norm_g"], w["w_spatial"],
                     w["b_spatial_t"], w["hgrn_norm_g"], w["w_proj_a"], w["w_proj_b"], w["w_out"],
                     w["norm2_g"])
    if v_bf is None:
        idx, gate, v_bf = _topk(hn2, w["peer_w_q"], w["peer_sub_keys"], table=w["peer_v"])
    else:
        idx, gate = _topk(hn2, w["peer_w_q"], w["peer_sub_keys"])
    p = _peer_up(hn2, w["peer_u"], idx)
    y = _peer_down(p, gate, idx, v_bf, x1, mod, mod_row, w["final_norm_g"])
    return y.reshape(x.shape), st_fin, v_bf


def kernel(x_prompt, x_sample, state_hgrn, c, c_ctx, w_ada, b_ada, norm1_g, w_in, sgu_norm_g,
           w_spatial, b_spatial, hgrn_lb, hgrn_norm_g, w_proj_a, w_proj_b, w_out, norm2_g,
           peer_w_q, peer_sub_keys, peer_u, peer_v, final_norm_g):
    assert w_in.shape[0] == 1, "single-layer stack"
    n_p, l_p, _ = x_prompt.shape
    n_s, l_s, _ = x_sample.shape
    assert n_s + 1 <= SUBLANES
    lb = jnp.cumsum(jax.nn.softmax(hgrn_lb.astype(F32), axis=0), axis=0)[0]

    cond = jnp.zeros((SUBLANES, D_MODEL), F32).at[0].set(c_ctx).at[1:1 + n_s].set(c)
    mod, w_in_bf = _adaln(cond, w_ada[0], b_ada[0][None], w_in[0])
    mod = mod.reshape(SUBLANES, N_MOD, D_MODEL)

    w = dict(
        norm1_g=norm1_g[0][None], w_in=w_in_bf, sgu_norm_g=sgu_norm_g[0][None],
        w_spatial=w_spatial[0].astype(BF16), b_spatial_t=b_spatial[0].T,
        hgrn_norm_g=hgrn_norm_g[0].reshape(1, H_B * DV),
        w_proj_a=w_proj_a[0].astype(BF16), w_proj_b=w_proj_b[0].astype(BF16),
        w_out=w_out[0].astype(BF16), norm2_g=norm2_g[0][None],
        peer_w_q=peer_w_q[0].astype(BF16),
        peer_sub_keys=peer_sub_keys[0].reshape(2 * PEER_HEADS, N_KEYS, PEER_QHALF).astype(BF16),
        peer_u=peer_u[0], peer_v=peer_v[0],
        final_norm_g=final_norm_g[None],
    )

    s0_lat = state_hgrn[:, 0].astype(F32)
    pos = _grid_pos_embed(l_s)

    y_prompt, s_ctx, v_bf = _trunk(x_prompt, None, mod, 0, False, None, lb, w, None)
    y_sample, _, _ = _trunk(x_sample, pos, mod, 1, True, s0_lat, lb, w, v_bf)
    return (y_prompt, y_sample, s_ctx[:, None])
```

```python
import functools

import jax
import jax.numpy as jnp
import numpy as np
from jax import lax
from jax.experimental import pallas as pl
from jax.experimental.pallas import tpu as pltpu

F32 = jnp.float32
BF16 = jnp.bfloat16
I32 = jnp.int32

D_MODEL = 1024
N_MOD = 6
A_WIDTH = 512
A_GROUPS = 4
A_GROUP_DIM = A_WIDTH // A_GROUPS
CHUNK_MLP = 128
H_B = 4
DK = 128
DV = 128
QK_WIDTH = H_B * DK
IN_WIDTH = 5632
COL_BLOCK = 512
PEER_HEADS = 8
N_KEYS = 128
KEY_BITS = 7
PEER_TOPK = 16
PEER_QHALF = 128
PEER_Q = 2 * PEER_HEADS * PEER_QHALF
N_EXPERTS = N_KEYS * N_KEYS
GRID_W = 64
POS_THETA = 10000.0
EPS = 1e-6

SUBLANES = 8
LANES = 128
VMEM_LIMIT_BYTES = 56 * 1024 * 1024

TB_IN = 512
CH = 128
TB_MERGE = 512
TB_TOPK = 512
TB_UP = 1024
TB_DOWN = 512
E_BLOCK = 1024
A_STRIDE = 136
SCATTER_UNROLL = 256
STREAM_AHEAD = 2
STREAM_SLOTS = STREAM_AHEAD + 1


def _sds(shape, dtype):
    return jax.ShapeDtypeStruct(shape, dtype)


def _params(*sem):
    return pltpu.CompilerParams(dimension_semantics=sem, vmem_limit_bytes=VMEM_LIMIT_BYTES)


def _dot(a, b):
    return jnp.dot(a, b, preferred_element_type=F32)


def _dot_nt(a, b):
    return lax.dot_general(a, b, (((1,), (1,)), ((), ())), preferred_element_type=F32)


def _dot_tn(a, b):
    return lax.dot_general(a, b, (((0,), (0,)), ((), ())), preferred_element_type=F32)


def _rms(x):
    return x * lax.rsqrt(jnp.mean(x * x, axis=-1, keepdims=True) + EPS)


def _adaln_kernel(c_ref, w_ref, b_ref, tab_ref, o_ref, tab_out_ref):
    a = jax.nn.silu(c_ref[...])
    o_ref[...] = jnp.dot(a, w_ref[...], preferred_element_type=F32,
                         precision=lax.Precision.HIGHEST) + b_ref[...]
    tab_out_ref[...] = tab_ref[...].astype(BF16)


def _adaln(cond8, w_ada, b_ada, table):
    n = w_ada.shape[1]
    tn = 1536
    steps = n // tn
    rows = table.shape[0] // steps
    assert rows * steps == table.shape[0] and rows % 16 == 0
    tab_spec = pl.BlockSpec((rows, table.shape[1]), lambda j: (j, 0))
    return pl.pallas_call(
        _adaln_kernel,
        grid=(steps,),
        in_specs=[pl.BlockSpec((SUBLANES, D_MODEL), lambda j: (0, 0)),
                  pl.BlockSpec((D_MODEL, tn), lambda j: (0, j)),
                  pl.BlockSpec((1, tn), lambda j: (0, j)),
                  tab_spec],
        out_specs=[pl.BlockSpec((SUBLANES, tn), lambda j: (0, j)), tab_spec],
        out_shape=[_sds((SUBLANES, n), F32), _sds(table.shape, BF16)],
        compiler_params=_params("arbitrary"),
        name="adaln",
    )(cond8, w_ada, b_ada, table)


def _inproj_kernel(*refs, add_pos):
    if add_pos:
        x_ref, pos_ref, mod_ref, g_ref, w_ref, z_ref = refs
    else:
        x_ref, mod_ref, g_ref, w_ref, z_ref = refs
    x = x_ref[...]
    if add_pos:
        x = x + pos_ref[...]
    h = _rms(x) * g_ref[...] * (1.0 + mod_ref[1:2, :]) + mod_ref[0:1, :]
    z_ref[...] = _dot(h.astype(BF16), w_ref[...])


def _inproj(x2d, pos, mod, mod_row, norm_g, w_in_bf, seq_len):
    t = x2d.shape[0]
    bps = seq_len // TB_IN
    add_pos = pos is not None
    in_specs = [pl.BlockSpec((TB_IN, D_MODEL), lambda i: (i, 0))]
    args = [x2d]
    if add_pos:
        in_specs.append(pl.BlockSpec((TB_IN, D_MODEL), lambda i: (i % bps, 0)))
        args.append(pos)
    in_specs += [pl.BlockSpec((None, N_MOD, D_MODEL), lambda i: (mod_row(i * TB_IN), 0, 0)),
                 pl.BlockSpec((1, D_MODEL), lambda i: (0, 0)),
                 pl.BlockSpec((D_MODEL, IN_WIDTH), lambda i: (0, 0), pipeline_mode=pl.Buffered(1))]
    args += [mod, norm_g, w_in_bf]
    return pl.pallas_call(
        functools.partial(_inproj_kernel, add_pos=add_pos),
        grid=(t // TB_IN,),
        in_specs=in_specs,
        out_specs=pl.BlockSpec((TB_IN, IN_WIDTH), lambda i: (i, 0)),
        out_shape=_sds((t, IN_WIDTH), F32),
        compiler_params=_params("arbitrary"),
        name="inproj",
    )(*args)


def _split3(x):
    hi = x.astype(BF16)
    r1 = x - hi.astype(F32)
    mid = r1.astype(BF16)
    lo = (r1 - mid.astype(F32)).astype(BF16)
    return hi, mid, lo


def _hgrn_direction(zq_ref, zf_ref, zv_ref, lb, st_scr, b_scr, k_scr, o_ref, reverse):
    f = lb + (1.0 - lb) * jax.nn.sigmoid(zf_ref[...])
    k_scr[...] = 1.0 - f
    lf = jnp.log(f)
    row = lax.broadcasted_iota(I32, (CH, CH), 0)
    col = lax.broadcasted_iota(I32, (CH, CH), 1)
    tri = jnp.where((row <= col) if reverse else (row >= col), 1.0, 0.0).astype(BF16)
    hi, mid, lo = _split3(lf)
    b_scr[...] = _dot(tri, hi) + _dot(tri, mid) + _dot(tri, lo)

    sub = lax.broadcasted_iota(I32, (SUBLANES, DK), 0)
    zero_tile = jnp.zeros((SUBLANES, DK), F32)
    n_tiles = CH // SUBLANES
    end_row = 0 if reverse else CH - 1

    for h in range(H_B):
        sl = slice(h * DK, (h + 1) * DK)
        q = zq_ref[:, sl]
        kk = k_scr[:, sl]
        bh = b_scr[:, sl]

        def masked(sc, m):
            if 2 * m >= CH:
                return sc
            shift = (2 * m).bit_length() - 1 if m else 0
            return jnp.where((row >> shift) == (col >> shift), sc, 0.0)

        s_tot = masked(_dot_nt(q.astype(BF16), kk.astype(BF16)), 0)

        m = 1
        while m < CH:
            q_tiles, k_tiles = [], []
            for j in range(n_tiles):
                r0 = j * SUBLANES
                rows = slice(r0, r0 + SUBLANES)
                if m < SUBLANES:
                    upper = (sub & m) != 0
                    gives_q = ~upper if reverse else upper
                    if m == 1:
                        scale_q = 1.0 - k_scr[rows, sl]
                        scale_k = None
                    else:
                        seam = m if reverse else m - 1
                        if 2 * m == SUBLANES:
                            b_mid = b_scr[r0 + seam:r0 + seam + 1, sl]
                        else:
                            b_mid = jnp.where(sub < 2 * m, b_scr[r0 + seam:r0 + seam + 1, sl],
                                              b_scr[r0 + 2 * m + seam:r0 + 2 * m + seam + 1, sl])
                        d = b_scr[rows, sl] - b_mid
                        scale_q = scale_k = jnp.exp(jnp.minimum(d, -d))
                    qt = zq_ref[rows, sl] * scale_q
                    kt = k_scr[rows, sl] if scale_k is None else k_scr[rows, sl] * scale_k
                    q_tiles.append(jnp.where(gives_q, qt, 0.0))
                    k_tiles.append(jnp.where(gives_q, 0.0, kt))
                else:
                    blk = r0 // m
                    gives_q = (blk % 2 == 0) if reverse else (blk % 2 == 1)
                    if gives_q:
                        ref_row = blk * m + m if reverse else blk * m - 1
                        q_tiles.append(zq_ref[rows, sl] * jnp.exp(
                            b_scr[rows, sl] - b_scr[ref_row:ref_row + 1, sl]))
                        k_tiles.append(zero_tile)
                    else:
                        ref_row = blk * m if reverse else blk * m + m - 1
                        k_tiles.append(k_scr[rows, sl] * jnp.exp(
                            b_scr[ref_row:ref_row + 1, sl] - b_scr[rows, sl]))
                        q_tiles.append(zero_tile)
            qm = jnp.concatenate(q_tiles, axis=0).astype(BF16)
            km = jnp.concatenate(k_tiles, axis=0).astype(BF16)
            s_tot = s_tot + masked(_dot_nt(qm, km), m)
            m *= 2

        v = zv_ref[:, sl].astype(BF16)
        st = st_scr[h]
        o_ref[:, sl] = (_dot(s_tot.astype(BF16), v)
                        + _dot_nt((q * jnp.exp(bh)).astype(BF16), st.astype(BF16)))
        b_end = b_scr[end_row:end_row + 1, sl]
        ke = (kk * jnp.exp(b_end - bh)).astype(BF16)
        st_scr[h] = st * jnp.exp(b_end) + _dot_tn(v, ke)


def _hgrn_kernel(*refs, n_chunk, has_s0):
    refs = list(refs)
    zqf_ref, zff_ref, zvf_ref, zqb_ref, zfb_ref, zvb_ref, lb_ref = refs[:7]
    s0_ref = refs[7] if has_s0 else None
    ofw_ref, obw_ref, sfin_ref, st_scr, b_scr, k_scr = refs[-6:]
    c = pl.program_id(1)

    @pl.when(c == 0)
    def _():
        for d in range(2):
            for h in range(H_B):
                st_scr[d, h] = s0_ref[d, h].T if has_s0 else jnp.zeros((DV, DK), F32)

    _hgrn_direction(zqf_ref, zff_ref, zvf_ref, lb_ref[0:1, :], st_scr.at[0], b_scr.at[0],
                    k_scr.at[0], ofw_ref, reverse=False)
    _hgrn_direction(zqb_ref, zfb_ref, zvb_ref, lb_ref[1:2, :], st_scr.at[1], b_scr.at[1],
                    k_scr.at[1], obw_ref, reverse=True)

    @pl.when(c == n_chunk - 1)
    def _():
        for d in range(2):
            for h in range(H_B):
                sfin_ref[d, h] = st_scr[d, h].T


def _hgrn(z, lb, s0, n_seq, seq_len):
    n_chunk = seq_len // CH
    t = n_seq * seq_len

    def fwd(col):
        return pl.BlockSpec((CH, COL_BLOCK), lambda s, c: (s * n_chunk + c, col))

    def bwd(col):
        return pl.BlockSpec((CH, COL_BLOCK), lambda s, c: (s * n_chunk + n_chunk - 1 - c, col))

    state_spec = pl.BlockSpec((None, 2, H_B, DK, DV), lambda s, c: (s, 0, 0, 0, 0))
    in_specs = [fwd(2), fwd(3), fwd(5), bwd(2), bwd(4), bwd(5),
                pl.BlockSpec((2, QK_WIDTH), lambda s, c: (0, 0))]
    args = [z, z, z, z, z, z, lb]
    if s0 is not None:
        in_specs.append(state_spec)
        args.append(s0)
    return pl.pallas_call(
        functools.partial(_hgrn_kernel, n_chunk=n_chunk, has_s0=s0 is not None),
        grid=(n_seq, n_chunk),
        in_specs=in_specs,
        out_specs=[fwd(0), bwd(0), state_spec],
        out_shape=[_sds((t, H_B * DV), F32), _sds((t, H_B * DV), F32),
                   _sds((n_seq, 2, H_B, DK, DV), F32)],
        scratch_shapes=[pltpu.VMEM((2, H_B, DV, DK), F32),
                        pltpu.VMEM((2, CH, QK_WIDTH), F32),
                        pltpu.VMEM((2, CH, QK_WIDTH), F32)],
        compiler_params=_params("arbitrary", "arbitrary"),
        name="hgrn",
    )(*args)


def _merge_kernel(*refs, add_pos):
    refs = list(refs)
    x_ref = refs.pop(0)
    pos_ref = refs.pop(0) if add_pos else None
    (zu_ref, zv_ref, zg_ref, za0_ref, za1_ref, zb0_ref, zb1_ref, ofw_ref, obw_ref, mod_ref,
     sgug_ref, ws_ref, bst_ref, hng_ref, wpa_ref, wpb_ref, wout_ref, n2g_ref,
     x1_ref, hn2_ref) = refs

    u = jax.nn.gelu(zu_ref[...])
    v = (_rms(jax.nn.gelu(zv_ref[...])) * sgug_ref[...]).astype(BF16)
    row_parts = []
    for ci in range(TB_MERGE // CHUNK_MLP):
        col_parts = []
        for g in range(A_GROUPS):
            vg = v[ci * CHUNK_MLP:(ci + 1) * CHUNK_MLP, g * A_GROUP_DIM:(g + 1) * A_GROUP_DIM]
            col_parts.append(_dot(ws_ref[g], vg) + bst_ref[:, g:g + 1])
        row_parts.append(jnp.concatenate(col_parts, axis=1))
    y_a = u * jnp.concatenate(row_parts, axis=0)

    o = ofw_ref[...] + obw_ref[...]
    on = jnp.concatenate([_rms(o[:, h * DV:(h + 1) * DV]) for h in range(H_B)], axis=1)
    y_b = on * hng_ref[...] * jax.nn.silu(zg_ref[...])

    pa = _dot(y_a.astype(BF16), wpa_ref[...])
    pb = _dot(y_b.astype(BF16), wpb_ref[...])
    za = jnp.concatenate([za0_ref[...], za1_ref[...]], axis=1)
    zb = jnp.concatenate([zb0_ref[...], zb1_ref[...]], axis=1)
    mix_in = jax.nn.sigmoid(za) * pa + jax.nn.sigmoid(zb) * pb
    mix = _dot(mix_in.astype(BF16), wout_ref[...])

    x = x_ref[...]
    if add_pos:
        x = x + pos_ref[...]
    x1 = x + mod_ref[2:3, :] * mix
    x1_ref[...] = x1
    hn2 = _rms(x1) * n2g_ref[...] * (1.0 + mod_ref[4:5, :]) + mod_ref[3:4, :]
    hn2_ref[...] = hn2.astype(BF16)


def _merge(x2d, pos, z, o_fw, o_bw, mod, mod_row, seq_len, sgu_g, ws_bf, bs_t, hgrn_g,
           wpa_bf, wpb_bf, wout_bf, norm2_g):
    t = x2d.shape[0]
    tb = TB_MERGE
    bps = seq_len // tb
    add_pos = pos is not None

    def zcol(k):
        return pl.BlockSpec((tb, COL_BLOCK), lambda i: (i, k))

    def full(shape):
        return pl.BlockSpec(shape, lambda i: (0,) * len(shape))

    in_specs = [pl.BlockSpec((tb, D_MODEL), lambda i: (i, 0))]
    args = [x2d]
    if add_pos:
        in_specs.append(pl.BlockSpec((tb, D_MODEL), lambda i: (i % bps, 0)))
        args.append(pos)
    in_specs += [zcol(0), zcol(1), zcol(6), zcol(7), zcol(8), zcol(9), zcol(10),
                 pl.BlockSpec((tb, A_WIDTH), lambda i: (i, 0)),
                 pl.BlockSpec((tb, A_WIDTH), lambda i: (i, 0)),
                 pl.BlockSpec((None, N_MOD, D_MODEL), lambda i: (mod_row(i * tb), 0, 0)),
                 full((1, A_WIDTH)), full((A_GROUPS, CHUNK_MLP, CHUNK_MLP)),
                 full((CHUNK_MLP, A_GROUPS)), full((1, H_B * DV)),
                 full((A_WIDTH, D_MODEL)), full((H_B * DV, D_MODEL)),
                 full((D_MODEL, D_MODEL)), full((1, D_MODEL))]
    args += [z] * 7 + [o_fw, o_bw, mod, sgu_g, ws_bf, bs_t, hgrn_g, wpa_bf, wpb_bf, wout_bf, norm2_g]
    return pl.pallas_call(
        functools.partial(_merge_kernel, add_pos=add_pos),
        grid=(t // tb,),
        in_specs=in_specs,
        out_specs=[pl.BlockSpec((tb, D_MODEL), lambda i: (i, 0)),
                   pl.BlockSpec((tb, D_MODEL), lambda i: (i, 0))],
        out_shape=[_sds((t, D_MODEL), F32), _sds((t, D_MODEL), BF16)],
        compiler_params=_params("arbitrary"),
        name="merge",
    )(*args)


def _sort_network(n):
    def merge(lo, hi, r):
        step = r * 2
        if step < hi - lo:
            yield from merge(lo, hi, step)
            yield from merge(lo + r, hi, step)
            yield from [(i, i + r) for i in range(lo + r, hi - r, step)]
        else:
            yield (lo, lo + r)

    def sort(lo, hi):
        if hi - lo >= 1:
            mid = lo + (hi - lo) // 2
            yield from sort(lo, mid)
            yield from sort(mid + 1, hi)
            yield from merge(lo, hi, 1)

    return tuple(sort(0, n - 1))


def _col_max(x):
    return jnp.max(x, axis=0, keepdims=True)


def _col_min(x):
    return jnp.min(x, axis=0, keepdims=True)


def _topk_kernel(*refs, cast_table):
    if cast_table:
        (hn_ref, wq_ref, keys_ref, tab_ref, idx_ref, g_ref, tab_out_ref,
         v_scr, i_scr, tv_scr, fp_scr, idxt_scr, gt_scr) = refs
        tab_out_ref[...] = tab_ref[...].astype(BF16)
    else:
        (hn_ref, wq_ref, keys_ref, idx_ref, g_ref,
         v_scr, i_scr, tv_scr, fp_scr, idxt_scr, gt_scr) = refs
    tb = hn_ref.shape[0]
    k = PEER_TOPK
    q = _dot(hn_ref[...], wq_ref[...]).astype(BF16)
    sub = lax.broadcasted_iota(I32, (SUBLANES, LANES), 0).astype(F32)
    n_tiles = N_KEYS // SUBLANES
    neg = -jnp.inf

    for h in range(PEER_HEADS):
        scores = [_dot_nt(keys_ref[2 * h + p], q[:, (2 * h + p) * PEER_QHALF:(2 * h + p + 1) * PEER_QHALF])
                  for p in range(2)]
        for g in range(tb // LANES):
            ln = slice(g * LANES, (g + 1) * LANES)
            for p in range(2):
                s = scores[p][:, ln]
                vals = [s[SUBLANES * v:SUBLANES * (v + 1), :] for v in range(n_tiles)]
                ids = [sub + float(SUBLANES * v) for v in range(n_tiles)]
                for i, j in _sort_network(n_tiles):
                    a, b, ia, ib = vals[i], vals[j], ids[i], ids[j]
                    swap = (b > a) | ((b == a) & (ib < ia))
                    vals[i], vals[j] = jnp.where(swap, b, a), jnp.where(swap, a, b)
                    ids[i], ids[j] = jnp.where(swap, ib, ia), jnp.where(swap, ia, ib)
                for r in range(k):
                    m = _col_max(vals[0])
                    first = _col_min(jnp.where(vals[0] == m, ids[0], float(N_KEYS)))
                    sel = ids[0] == first
                    v_scr[p, r:r + 1, ln] = m
                    i_scr[p, r:r + 1, ln] = first
                    for d in range(k - 1 - r):
                        vals[d] = jnp.where(sel, vals[d + 1], vals[d])
                        ids[d] = jnp.where(sel, ids[d + 1], ids[d])

            lists = []
            for r in range(k):
                val = v_scr[0, r:r + 1, ln] + v_scr[1, 0:SUBLANES, ln]
                n_valid = k // (r + 1)
                lists.append(val if n_valid >= SUBLANES else jnp.where(sub < float(n_valid), val, neg))
            extra = v_scr[0, 0:1, ln] + v_scr[1, SUBLANES:k, ln]
            flat = sub
            flat_extra = sub + float(SUBLANES)
            for j in range(k):
                m = _col_max(jnp.maximum(lists[0], extra))
                first = _col_min(jnp.minimum(jnp.where(lists[0] == m, flat, float(k * k)),
                                             jnp.where(extra == m, flat_extra, float(k * k))))
                tv_scr[j:j + 1, ln] = m
                fp_scr[j:j + 1, ln] = first
                left = k - 1 - j
                if left:
                    sel = flat == first
                    for d in range(left):
                        lists[d] = jnp.where(sel, lists[d + 1], lists[d])
                    flat = jnp.where(sel, flat + float(k), flat)
                    extra = jnp.where(flat_extra == first, neg, extra)

        fp = fp_scr[...]
        pr = jnp.floor(fp * (1.0 / k))
        pc = fp - pr * float(k)
        e1 = jnp.zeros_like(fp)
        e2 = jnp.zeros_like(fp)
        for r in range(k):
            e1 = jnp.where(pr == float(r), i_scr[0, r:r + 1, :], e1)
            e2 = jnp.where(pc == float(r), i_scr[1, r:r + 1, :], e2)
        idxt_scr[h * k:(h + 1) * k, :] = (e1 * float(N_KEYS) + e2).astype(I32)
        tv = tv_scr[...]
        ex = jnp.exp(tv - tv[0:1, :])
        gt_scr[h * k:(h + 1) * k, :] = ex / jnp.sum(ex, axis=0, keepdims=True)

    idx_ref[...] = idxt_scr[...].T
    g_ref[...] = gt_scr[...].T


def _topk(hn2, wq_bf, keys_bf, table=None):
    t = hn2.shape[0]
    tb = TB_TOPK
    hk = PEER_HEADS * PEER_TOPK
    steps = t // tb
    in_specs = [pl.BlockSpec((tb, D_MODEL), lambda i: (i, 0)),
                pl.BlockSpec((D_MODEL, PEER_Q), lambda i: (0, 0)),
                pl.BlockSpec((2 * PEER_HEADS, N_KEYS, PEER_QHALF), lambda i: (0, 0, 0))]
    out_specs = [pl.BlockSpec((tb, hk), lambda i: (i, 0)), pl.BlockSpec((tb, hk), lambda i: (i, 0))]
    out_shape = [_sds((t, hk), I32), _sds((t, hk), F32)]
    args = [hn2, wq_bf, keys_bf]
    if table is not None:
        rows = table.shape[0] // steps
        assert rows * steps == table.shape[0] and rows % 16 == 0
        in_specs.append(pl.BlockSpec((rows, table.shape[1]), lambda i: (i, 0)))
        out_specs.append(pl.BlockSpec((rows, table.shape[1]), lambda i: (i, 0)))
        out_shape.append(_sds(table.shape, BF16))
        args.append(table)
    return pl.pallas_call(
        functools.partial(_topk_kernel, cast_table=table is not None),
        grid=(steps,),
        in_specs=in_specs,
        out_specs=out_specs,
        out_shape=out_shape,
        scratch_shapes=[pltpu.VMEM((2, PEER_TOPK, tb), F32),
                        pltpu.VMEM((2, PEER_TOPK, tb), F32),
                        pltpu.VMEM((PEER_TOPK, tb), F32),
                        pltpu.VMEM((PEER_TOPK, tb), F32),
                        pltpu.VMEM((hk, tb), I32),
                        pltpu.VMEM((hk, tb), F32)],
        compiler_params=_params("arbitrary"),
        name="peer_topk",
    )(*args)


def _stream_copy(src_hbm, buf, sem, step, block):
    slot = lax.rem(step, STREAM_SLOTS)
    return pltpu.make_async_copy(src_hbm.at[pl.ds(block * E_BLOCK, E_BLOCK)], buf.at[slot], sem.at[slot])


def _stream_prefetch(copy_for_step, step, total):
    @pl.when(step == 0)
    def _():
        for k in range(STREAM_AHEAD):
            copy_for_step(k).start()

    @pl.when(step + STREAM_AHEAD < total)
    def _():
        copy_for_step(step + STREAM_AHEAD).start()


def _peer_up_kernel(hn_ref, u_hbm, idx_ref, p_ref, s_even, s_odd, ubuf, usem):
    s = pl.program_id(0)
    total = pl.num_programs(0)
    n_blocks = N_EXPERTS // E_BLOCK
    per = E_BLOCK // N_KEYS

    def u_copy(step):
        return _stream_copy(u_hbm, ubuf, usem, step, lax.rem(jnp.minimum(step, total - 2), n_blocks))

    _stream_prefetch(u_copy, s, total)

    @pl.when(s == 0)
    def _():
        p_ref[...] = jnp.zeros_like(p_ref)
        s_odd[...] = jnp.zeros_like(s_odd)

    u_copy(s).wait()
    slot = lax.rem(s, STREAM_SLOTS)
    prev_step = jnp.maximum(s - 1, 0)
    row0 = pl.multiple_of(lax.div(prev_step, n_blocks) * TB_UP, TB_UP)
    first_row = jnp.where(s == 0, -per, lax.rem(prev_step, n_blocks) * per)

    def step(cur, prev):
        cur[...] = _dot_nt(hn_ref[...], ubuf[slot].astype(BF16))
        for c in range(TB_UP // SUBLANES):
            rows = pl.ds(row0 + c * SUBLANES, SUBLANES)
            idx = idx_ref[rows, :]
            i1 = idx >> KEY_BITS
            i2 = idx & (N_KEYS - 1)
            acc = p_ref[rows, :]
            for k in range(per):
                picked = jnp.take_along_axis(
                    prev[c * SUBLANES:(c + 1) * SUBLANES, k * N_KEYS:(k + 1) * N_KEYS], i2, axis=1)
                acc = jnp.where(i1 == first_row + k, picked, acc)
            p_ref[rows, :] = acc

    @pl.when(s % 2 == 0)
    def _():
        step(s_even, s_odd)

    @pl.when(s % 2 == 1)
    def _():
        step(s_odd, s_even)


def _peer_up(hn2, u, idx):
    t = hn2.shape[0]
    hk = PEER_HEADS * PEER_TOPK
    n_blocks = N_EXPERTS // E_BLOCK
    nb = t // TB_UP
    whole = pl.BlockSpec((t, hk), lambda s: (0, 0))
    return pl.pallas_call(
        _peer_up_kernel,
        grid=(nb * n_blocks + 1,),
        in_specs=[pl.BlockSpec((TB_UP, D_MODEL), lambda s: (jnp.minimum(s // n_blocks, nb - 1), 0)),
                  pl.BlockSpec(memory_space=pl.ANY),
                  whole],
        out_specs=whole,
        out_shape=_sds((t, hk), F32),
        scratch_shapes=[pltpu.VMEM((TB_UP, E_BLOCK), F32), pltpu.VMEM((TB_UP, E_BLOCK), F32),
                        pltpu.VMEM((STREAM_SLOTS, E_BLOCK, D_MODEL), u.dtype),
                        pltpu.SemaphoreType.DMA((STREAM_SLOTS,))],
        compiler_params=_params("arbitrary"),
        name="peer_up",
    )(hn2, u, idx)


def _peer_down_kernel(p_ref, g_ref, idx_ref, v_hbm, x1_ref, mod_ref, fg_ref, y_ref, a_scr, abuf, vbuf, vsem):
    j = pl.program_id(1)
    nj = pl.num_programs(1)
    step_id = pl.program_id(0) * nj + j

    def v_copy(s):
        return _stream_copy(v_hbm, vbuf, vsem, s, lax.rem(s, nj))

    _stream_prefetch(v_copy, step_id, pl.num_programs(0) * nj)
    v_copy(step_id).wait()

    @pl.when(j == 0)
    def _():
        y_ref[...] = jnp.zeros_like(y_ref)
        a_scr[...] = jax.nn.gelu(p_ref[...]) * g_ref[...]
        key_ids = lax.broadcasted_iota(I32, (N_KEYS, PEER_HEADS * PEER_TOPK), 0)

        def body(tt, carry):
            for u in range(SCATTER_UNROLL):
                t = tt * SCATTER_UNROLL + u
                idx = idx_ref[pl.ds(t, 1), :]
                pm = jnp.where((idx >> KEY_BITS) == key_ids, a_scr[pl.ds(t, 1), :], 0.0).astype(BF16)
                qm = jnp.where((idx & (N_KEYS - 1)) == key_ids, 1.0, 0.0).astype(BF16)
                abuf[pl.ds(pl.multiple_of(t * A_STRIDE, SUBLANES), N_KEYS), :] = _dot_nt(pm, qm)
            return carry

        lax.fori_loop(0, TB_DOWN // SCATTER_UNROLL, body, 0)

    per = E_BLOCK // N_KEYS
    cols = [abuf[pl.ds(j * per + k, TB_DOWN, stride=A_STRIDE), :] for k in range(per)]
    a_blk = jnp.concatenate(cols, axis=1).astype(BF16)
    y_ref[...] += _dot(a_blk, vbuf[lax.rem(step_id, STREAM_SLOTS)])

    @pl.when(j == nj - 1)
    def _():
        x2 = x1_ref[...] + mod_ref[5:6, :] * y_ref[...]
        y_ref[...] = _rms(x2) * fg_ref[...]


def _peer_down(p, g, idx, v_bf, x1, mod, mod_row, final_g):
    t = p.shape[0]
    hk = PEER_HEADS * PEER_TOPK
    tok = pl.BlockSpec((TB_DOWN, hk), lambda i, j: (i, 0))
    wide = pl.BlockSpec((TB_DOWN, D_MODEL), lambda i, j: (i, 0))
    return pl.pallas_call(
        _peer_down_kernel,
        grid=(t // TB_DOWN, N_EXPERTS // E_BLOCK),
        in_specs=[tok, tok, tok, pl.BlockSpec(memory_space=pl.ANY), wide,
                  pl.BlockSpec((None, N_MOD, D_MODEL), lambda i, j: (mod_row(i * TB_DOWN), 0, 0)),
                  pl.BlockSpec((1, D_MODEL), lambda i, j: (0, 0))],
        out_specs=wide,
        out_shape=_sds((t, D_MODEL), F32),
        scratch_shapes=[pltpu.VMEM((TB_DOWN, hk), F32),
                        pltpu.VMEM((TB_DOWN * A_STRIDE, N_KEYS), F32),
                        pltpu.VMEM((STREAM_SLOTS, E_BLOCK, D_MODEL), v_bf.dtype),
                        pltpu.SemaphoreType.DMA((STREAM_SLOTS,))],
        compiler_params=_params("arbitrary", "arbitrary"),
        name="peer_down",
    )(p, g, idx, v_bf, x1, mod, final_g)


def _grid_pos_embed(n_tokens):
    rows = n_tokens // GRID_W
    r = np.repeat(np.arange(rows, dtype=np.float64), GRID_W)
    col = np.tile(np.arange(GRID_W, dtype=np.float64), rows)
    quarter = D_MODEL // 4
    omega = 1.0 / (POS_THETA ** (np.arange(quarter, dtype=np.float64) / quarter))
    ar = r[:, None] * omega
    ac = col[:, None] * omega
    table = np.concatenate([np.sin(ar), np.cos(ar), np.sin(ac), np.cos(ac)], axis=-1)
    return jnp.asarray(table.astype(np.float32))


def _trunk(x, pos, mod, mod_base, per_seq, s0, lb, w, v_bf):
    n_seq, seq_len, _ = x.shape
    x2d = x.reshape(n_seq * seq_len, D_MODEL)

    def mod_row(tok):
        return mod_base + (tok // seq_len if per_seq else 0)

    z = _inproj(x2d, pos, mod, mod_row, w["norm1_g"], w["w_in"], seq_len)
    o_fw, o_bw, st_fin = _hgrn(z, lb, s0, n_seq, seq_len)
    x1, hn2 = _merge(x2d, pos, z, o_fw, o_bw, mod, mod_row, seq_len, w["sgu_norm_g"], w["w_spatial"],
                     w["b_spatial_t"], w["hgrn_norm_g"], w["w_proj_a"], w["w_proj_b"], w["w_out"],
                     w["norm2_g"])
    if v_bf is None:
        idx, gate, v_bf = _topk(hn2, w["peer_w_q"], w["peer_sub_keys"], table=w["peer_v"])
    else:
        idx, gate = _topk(hn2, w["peer_w_q"], w["peer_sub_keys"])
    p = _peer_up(hn2, w["peer_u"], idx)
    y = _peer_down(p, gate, idx, v_bf, x1, mod, mod_row, w["final_norm_g"])
    return y.reshape(x.shape), st_fin, v_bf


def kernel(x_prompt, x_sample, state_hgrn, c, c_ctx, w_ada, b_ada, norm1_g, w_in, sgu_norm_g,
           w_spatial, b_spatial, hgrn_lb, hgrn_norm_g, w_proj_a, w_proj_b, w_out, norm2_g,
           peer_w_q, peer_sub_keys, peer_u, peer_v, final_norm_g):
    assert w_in.shape[0] == 1, "single-layer stack"
    n_p, l_p, _ = x_prompt.shape
    n_s, l_s, _ = x_sample.shape
    assert n_s + 1 <= SUBLANES
    lb = jnp.cumsum(jax.nn.softmax(hgrn_lb.astype(F32), axis=0), axis=0)[0]

    cond = jnp.zeros((SUBLANES, D_MODEL), F32).at[0].set(c_ctx).at[1:1 + n_s].set(c)
    mod, w_in_bf = _adaln(cond, w_ada[0], b_ada[0][None], w_in[0])
    mod = mod.reshape(SUBLANES, N_MOD, D_MODEL)

    w = dict(
        norm1_g=norm1_g[0][None], w_in=w_in_bf, sgu_norm_g=sgu_norm_g[0][None],
        w_spatial=w_spatial[0].astype(BF16), b_spatial_t=b_spatial[0].T,
        hgrn_norm_g=hgrn_norm_g[0].reshape(1, H_B * DV),
        w_proj_a=w_proj_a[0].astype(BF16), w_proj_b=w_proj_b[0].astype(BF16),
        w_out=w_out[0].astype(BF16), norm2_g=norm2_g[0][None],
        peer_w_q=peer_w_q[0].astype(BF16),
        peer_sub_keys=peer_sub_keys[0].reshape(2 * PEER_HEADS, N_KEYS, PEER_QHALF).astype(BF16),
        peer_u=peer_u[0], peer_v=peer_v[0],
        final_norm_g=final_norm_g[None],
    )

    s0_lat = state_hgrn[:, 0].astype(F32)
    pos = _grid_pos_embed(l_s)

    y_prompt, s_ctx, v_bf = _trunk(x_prompt, None, mod, 0, False, None, lb, w, None)
    y_sample, _, _ = _trunk(x_sample, pos, mod, 1, True, s0_lat, lb, w, v_bf)
    return (y_prompt, y_sample, s_ctx[:, None])
```

```python
import functools

import jax
import jax.numpy as jnp
import numpy as np
from jax import lax
from jax.experimental import pallas as pl
from jax.experimental.pallas import tpu as pltpu

F32 = jnp.float32
BF16 = jnp.bfloat16
I32 = jnp.int32

D_MODEL = 1024
N_MOD = 6
A_WIDTH = 512
A_GROUPS = 4
A_GROUP_DIM = A_WIDTH // A_GROUPS
CHUNK_MLP = 128
H_B = 4
DK = 128
DV = 128
QK_WIDTH = H_B * DK
IN_WIDTH = 5632
COL_BLOCK = 512
PEER_HEADS = 8
N_KEYS = 128
KEY_BITS = 7
PEER_TOPK = 16
PEER_QHALF = 128
PEER_Q = 2 * PEER_HEADS * PEER_QHALF
N_EXPERTS = N_KEYS * N_KEYS
GRID_W = 64
POS_THETA = 10000.0
EPS = 1e-6

SUBLANES = 8
LANES = 128
VMEM_LIMIT_BYTES = 56 * 1024 * 1024

TB_IN = 512
CH = 128
TB_MERGE = 512
TB_TOPK = 256
TB_UP = 1024
TB_DOWN = 512
E_BLOCK = 1024
A_STRIDE = 136
SCATTER_UNROLL = 128
STREAM_AHEAD = 2
STREAM_SLOTS = STREAM_AHEAD + 1


def _sds(shape, dtype):
    return jax.ShapeDtypeStruct(shape, dtype)


def _params(*sem):
    return pltpu.CompilerParams(dimension_semantics=sem, vmem_limit_bytes=VMEM_LIMIT_BYTES)


def _dot(a, b):
    return jnp.dot(a, b, preferred_element_type=F32)


def _dot_nt(a, b):
    return lax.dot_general(a, b, (((1,), (1,)), ((), ())), preferred_element_type=F32)


def _dot_tn(a, b):
    return lax.dot_general(a, b, (((0,), (0,)), ((), ())), preferred_element_type=F32)


def _rms(x):
    return x * lax.rsqrt(jnp.mean(x * x, axis=-1, keepdims=True) + EPS)


def _adaln_kernel(c_ref, w_ref, b_ref, tab_ref, o_ref, tab_out_ref):
    a = jax.nn.silu(c_ref[...])
    o_ref[...] = jnp.dot(a, w_ref[...], preferred_element_type=F32,
                         precision=lax.Precision.HIGHEST) + b_ref[...]
    tab_out_ref[...] = tab_ref[...].astype(BF16)


def _adaln(cond8, w_ada, b_ada, table):
    n = w_ada.shape[1]
    tn = 1536
    steps = n // tn
    rows = table.shape[0] // steps
    assert rows * steps == table.shape[0] and rows % 16 == 0
    tab_spec = pl.BlockSpec((rows, table.shape[1]), lambda j: (j, 0))
    return pl.pallas_call(
        _adaln_kernel,
        grid=(steps,),
        in_specs=[pl.BlockSpec((SUBLANES, D_MODEL), lambda j: (0, 0)),
                  pl.BlockSpec((D_MODEL, tn), lambda j: (0, j)),
                  pl.BlockSpec((1, tn), lambda j: (0, j)),
                  tab_spec],
        out_specs=[pl.BlockSpec((SUBLANES, tn), lambda j: (0, j)), tab_spec],
        out_shape=[_sds((SUBLANES, n), F32), _sds(table.shape, BF16)],
        compiler_params=_params("arbitrary"),
        name="adaln",
    )(cond8, w_ada, b_ada, table)


def _inproj_kernel(*refs, add_pos):
    if add_pos:
        x_ref, pos_ref, mod_ref, g_ref, w_ref, z_ref = refs
    else:
        x_ref, mod_ref, g_ref, w_ref, z_ref = refs
    x = x_ref[...]
    if add_pos:
        x = x + pos_ref[...]
    h = _rms(x) * g_ref[...] * (1.0 + mod_ref[1:2, :]) + mod_ref[0:1, :]
    z_ref[...] = _dot(h.astype(BF16), w_ref[...])


def _inproj(x2d, pos, mod, mod_row, norm_g, w_in_bf, seq_len):
    t = x2d.shape[0]
    bps = seq_len // TB_IN
    add_pos = pos is not None
    in_specs = [pl.BlockSpec((TB_IN, D_MODEL), lambda i: (i, 0))]
    args = [x2d]
    if add_pos:
        in_specs.append(pl.BlockSpec((TB_IN, D_MODEL), lambda i: (i % bps, 0)))
        args.append(pos)
    in_specs += [pl.BlockSpec((None, N_MOD, D_MODEL), lambda i: (mod_row(i * TB_IN), 0, 0)),
                 pl.BlockSpec((1, D_MODEL), lambda i: (0, 0)),
                 pl.BlockSpec((D_MODEL, IN_WIDTH), lambda i: (0, 0), pipeline_mode=pl.Buffered(1))]
    args += [mod, norm_g, w_in_bf]
    return pl.pallas_call(
        functools.partial(_inproj_kernel, add_pos=add_pos),
        grid=(t // TB_IN,),
        in_specs=in_specs,
        out_specs=pl.BlockSpec((TB_IN, IN_WIDTH), lambda i: (i, 0)),
        out_shape=_sds((t, IN_WIDTH), F32),
        compiler_params=_params("arbitrary"),
        name="inproj",
    )(*args)


def _split3(x):
    hi = x.astype(BF16)
    r1 = x - hi.astype(F32)
    mid = r1.astype(BF16)
    lo = (r1 - mid.astype(F32)).astype(BF16)
    return hi, mid, lo


def _hgrn_direction(zq_ref, zf_ref, zv_ref, lb, st_scr, b_scr, k_scr, o_ref, reverse):
    f = lb + (1.0 - lb) * jax.nn.sigmoid(zf_ref[...])
    k_scr[...] = 1.0 - f
    lf = jnp.log(f)
    row = lax.broadcasted_iota(I32, (CH, CH), 0)
    col = lax.broadcasted_iota(I32, (CH, CH), 1)
    tri = jnp.where((row <= col) if reverse else (row >= col), 1.0, 0.0).astype(BF16)
    hi, mid, lo = _split3(lf)
    b_scr[...] = _dot(tri, hi) + _dot(tri, mid) + _dot(tri, lo)

    sub = lax.broadcasted_iota(I32, (SUBLANES, DK), 0)
    zero_tile = jnp.zeros((SUBLANES, DK), F32)
    n_tiles = CH // SUBLANES
    end_row = 0 if reverse else CH - 1

    def level_tiles(h, m):
        sl = slice(h * DK, (h + 1) * DK)
        q_tiles, k_tiles = [], []
        for j in range(n_tiles):
            r0 = j * SUBLANES
            rows = slice(r0, r0 + SUBLANES)
            if m < SUBLANES:
                upper = (sub & m) != 0
                gives_q = ~upper if reverse else upper
                if m == 1:
                    scale_q = 1.0 - k_scr[rows, sl]
                    scale_k = None
                else:
                    seam = m if reverse else m - 1
                    if 2 * m == SUBLANES:
                        b_mid = b_scr[r0 + seam:r0 + seam + 1, sl]
                    else:
                        b_mid = jnp.where(sub < 2 * m, b_scr[r0 + seam:r0 + seam + 1, sl],
                                          b_scr[r0 + 2 * m + seam:r0 + 2 * m + seam + 1, sl])
                    d = b_scr[rows, sl] - b_mid
                    scale_q = scale_k = jnp.exp(jnp.minimum(d, -d))
                qt = zq_ref[rows, sl] * scale_q
                kt = k_scr[rows, sl] if scale_k is None else k_scr[rows, sl] * scale_k
                q_tiles.append(jnp.where(gives_q, qt, 0.0))
                k_tiles.append(jnp.where(gives_q, 0.0, kt))
            else:
                blk = r0 // m
                gives_q = (blk % 2 == 0) if reverse else (blk % 2 == 1)
                if gives_q:
                    ref_row = blk * m + m if reverse else blk * m - 1
                    q_tiles.append(zq_ref[rows, sl] * jnp.exp(
                        b_scr[rows, sl] - b_scr[ref_row:ref_row + 1, sl]))
                    k_tiles.append(zero_tile)
                else:
                    ref_row = blk * m if reverse else blk * m + m - 1
                    k_tiles.append(k_scr[rows, sl] * jnp.exp(
                        b_scr[ref_row:ref_row + 1, sl] - b_scr[rows, sl]))
                    q_tiles.append(zero_tile)
        return q_tiles, k_tiles

    def masked(sc, m):
        if 2 * m >= CH:
            return sc
        shift = (2 * m).bit_length() - 1 if m else 0
        return jnp.where((row >> shift) == (col >> shift), sc, 0.0)

    for h0 in range(0, H_B, 2):
        pair = (h0, h0 + 1)
        sls = [slice(h * DK, (h + 1) * DK) for h in pair]
        q2 = jnp.concatenate([zq_ref[:, sl] for sl in sls], axis=0).astype(BF16)
        k2 = jnp.concatenate([k_scr[:, sl] for sl in sls], axis=0).astype(BF16)
        sc2 = _dot_nt(q2, k2)
        s_tot = [masked(sc2[i * CH:(i + 1) * CH, i * CH:(i + 1) * CH], 0) for i in range(2)]
        m = 1
        while m < CH:
            tiles = [level_tiles(h, m) for h in pair]
            qm = jnp.concatenate(tiles[0][0] + tiles[1][0], axis=0).astype(BF16)
            km = jnp.concatenate(tiles[0][1] + tiles[1][1], axis=0).astype(BF16)
            sc2 = _dot_nt(qm, km)
            for i in range(2):
                s_tot[i] = s_tot[i] + masked(sc2[i * CH:(i + 1) * CH, i * CH:(i + 1) * CH], m)
            m *= 2

        for i, h in enumerate(pair):
            sl = sls[i]
            q = zq_ref[:, sl]
            kk = k_scr[:, sl]
            bh = b_scr[:, sl]
            v = zv_ref[:, sl].astype(BF16)
            st = st_scr[h]
            o_ref[:, sl] = (_dot(s_tot[i].astype(BF16), v)
                            + _dot_nt((q * jnp.exp(bh)).astype(BF16), st.astype(BF16)))
            b_end = b_scr[end_row:end_row + 1, sl]
            ke = (kk * jnp.exp(b_end - bh)).astype(BF16)
            st_scr[h] = st * jnp.exp(b_end) + _dot_tn(v, ke)


def _hgrn_kernel(*refs, n_chunk, has_s0):
    refs = list(refs)
    zqf_ref, zff_ref, zvf_ref, zqb_ref, zfb_ref, zvb_ref, lb_ref = refs[:7]
    s0_ref = refs[7] if has_s0 else None
    ofw_ref, obw_ref, sfin_ref, st_scr, b_scr, k_scr = refs[-6:]
    c = pl.program_id(1)

    @pl.when(c == 0)
    def _():
        for d in range(2):
            for h in range(H_B):
                st_scr[d, h] = s0_ref[d, h].T if has_s0 else jnp.zeros((DV, DK), F32)

    _hgrn_direction(zqf_ref, zff_ref, zvf_ref, lb_ref[0:1, :], st_scr.at[0], b_scr.at[0],
                    k_scr.at[0], ofw_ref, reverse=False)
    _hgrn_direction(zqb_ref, zfb_ref, zvb_ref, lb_ref[1:2, :], st_scr.at[1], b_scr.at[1],
                    k_scr.at[1], obw_ref, reverse=True)

    @pl.when(c == n_chunk - 1)
    def _():
        for d in range(2):
            for h in range(H_B):
                sfin_ref[d, h] = st_scr[d, h].T


def _hgrn(z, lb, s0, n_seq, seq_len):
    n_chunk = seq_len // CH
    t = n_seq * seq_len

    def fwd(col):
        return pl.BlockSpec((CH, COL_BLOCK), lambda s, c: (s * n_chunk + c, col))

    def bwd(col):
        return pl.BlockSpec((CH, COL_BLOCK), lambda s, c: (s * n_chunk + n_chunk - 1 - c, col))

    state_spec = pl.BlockSpec((None, 2, H_B, DK, DV), lambda s, c: (s, 0, 0, 0, 0))
    in_specs = [fwd(2), fwd(3), fwd(5), bwd(2), bwd(4), bwd(5),
                pl.BlockSpec((2, QK_WIDTH), lambda s, c: (0, 0))]
    args = [z, z, z, z, z, z, lb]
    if s0 is not None:
        in_specs.append(state_spec)
        args.append(s0)
    return pl.pallas_call(
        functools.partial(_hgrn_kernel, n_chunk=n_chunk, has_s0=s0 is not None),
        grid=(n_seq, n_chunk),
        in_specs=in_specs,
        out_specs=[fwd(0), bwd(0), state_spec],
        out_shape=[_sds((t, H_B * DV), F32), _sds((t, H_B * DV), F32),
                   _sds((n_seq, 2, H_B, DK, DV), F32)],
        scratch_shapes=[pltpu.VMEM((2, H_B, DV, DK), F32),
                        pltpu.VMEM((2, CH, QK_WIDTH), F32),
                        pltpu.VMEM((2, CH, QK_WIDTH), F32)],
        compiler_params=_params("arbitrary", "arbitrary"),
        name="hgrn",
    )(*args)


def _merge_kernel(*refs, add_pos):
    refs = list(refs)
    x_ref = refs.pop(0)
    pos_ref = refs.pop(0) if add_pos else None
    (zu_ref, zv_ref, zg_ref, za0_ref, za1_ref, zb0_ref, zb1_ref, ofw_ref, obw_ref, mod_ref,
     sgug_ref, ws_ref, bst_ref, hng_ref, wpa_ref, wpb_ref, wout_ref, n2g_ref,
     x1_ref, hn2_ref) = refs

    u = jax.nn.gelu(zu_ref[...])
    v = (_rms(jax.nn.gelu(zv_ref[...])) * sgug_ref[...]).astype(BF16)
    row_parts = []
    for ci in range(TB_MERGE // CHUNK_MLP):
        col_parts = []
        for g in range(A_GROUPS):
            vg = v[ci * CHUNK_MLP:(ci + 1) * CHUNK_MLP, g * A_GROUP_DIM:(g + 1) * A_GROUP_DIM]
            col_parts.append(_dot(ws_ref[g], vg) + bst_ref[:, g:g + 1])
        row_parts.append(jnp.concatenate(col_parts, axis=1))
    y_a = u * jnp.concatenate(row_parts, axis=0)

    o = ofw_ref[...] + obw_ref[...]
    on = jnp.concatenate([_rms(o[:, h * DV:(h + 1) * DV]) for h in range(H_B)], axis=1)
    y_b = on * hng_ref[...] * jax.nn.silu(zg_ref[...])

    pa = _dot(y_a.astype(BF16), wpa_ref[...])
    pb = _dot(y_b.astype(BF16), wpb_ref[...])
    za = jnp.concatenate([za0_ref[...], za1_ref[...]], axis=1)
    zb = jnp.concatenate([zb0_ref[...], zb1_ref[...]], axis=1)
    mix_in = jax.nn.sigmoid(za) * pa + jax.nn.sigmoid(zb) * pb
    mix = _dot(mix_in.astype(BF16), wout_ref[...])

    x = x_ref[...]
    if add_pos:
        x = x + pos_ref[...]
    x1 = x + mod_ref[2:3, :] * mix
    x1_ref[...] = x1
    hn2 = _rms(x1) * n2g_ref[...] * (1.0 + mod_ref[4:5, :]) + mod_ref[3:4, :]
    hn2_ref[...] = hn2.astype(BF16)


def _merge(x2d, pos, z, o_fw, o_bw, mod, mod_row, seq_len, sgu_g, ws_bf, bs_t, hgrn_g,
           wpa_bf, wpb_bf, wout_bf, norm2_g):
    t = x2d.shape[0]
    tb = TB_MERGE
    bps = seq_len // tb
    add_pos = pos is not None

    def zcol(k):
        return pl.BlockSpec((tb, COL_BLOCK), lambda i: (i, k))

    def full(shape):
        return pl.BlockSpec(shape, lambda i: (0,) * len(shape))

    in_specs = [pl.BlockSpec((tb, D_MODEL), lambda i: (i, 0))]
    args = [x2d]
    if add_pos:
        in_specs.append(pl.BlockSpec((tb, D_MODEL), lambda i: (i % bps, 0)))
        args.append(pos)
    in_specs += [zcol(0), zcol(1), zcol(6), zcol(7), zcol(8), zcol(9), zcol(10),
                 pl.BlockSpec((tb, A_WIDTH), lambda i: (i, 0)),
                 pl.BlockSpec((tb, A_WIDTH), lambda i: (i, 0)),
                 pl.BlockSpec((None, N_MOD, D_MODEL), lambda i: (mod_row(i * tb), 0, 0)),
                 full((1, A_WIDTH)), full((A_GROUPS, CHUNK_MLP, CHUNK_MLP)),
                 full((CHUNK_MLP, A_GROUPS)), full((1, H_B * DV)),
                 full((A_WIDTH, D_MODEL)), full((H_B * DV, D_MODEL)),
                 full((D_MODEL, D_MODEL)), full((1, D_MODEL))]
    args += [z] * 7 + [o_fw, o_bw, mod, sgu_g, ws_bf, bs_t, hgrn_g, wpa_bf, wpb_bf, wout_bf, norm2_g]
    return pl.pallas_call(
        functools.partial(_merge_kernel, add_pos=add_pos),
        grid=(t // tb,),
        in_specs=in_specs,
        out_specs=[pl.BlockSpec((tb, D_MODEL), lambda i: (i, 0)),
                   pl.BlockSpec((tb, D_MODEL), lambda i: (i, 0))],
        out_shape=[_sds((t, D_MODEL), F32), _sds((t, D_MODEL), BF16)],
        compiler_params=_params("arbitrary"),
        name="merge",
    )(*args)


def _sort_network(n):
    def merge(lo, hi, r):
        step = r * 2
        if step < hi - lo:
            yield from merge(lo, hi, step)
            yield from merge(lo + r, hi, step)
            yield from [(i, i + r) for i in range(lo + r, hi - r, step)]
        else:
            yield (lo, lo + r)

    def sort(lo, hi):
        if hi - lo >= 1:
            mid = lo + (hi - lo) // 2
            yield from sort(lo, mid)
            yield from sort(mid + 1, hi)
            yield from merge(lo, hi, 1)

    return tuple(sort(0, n - 1))


def _col_max(x):
    return jnp.max(x, axis=0, keepdims=True)


def _col_min(x):
    return jnp.min(x, axis=0, keepdims=True)


def _topk_kernel(*refs, cast_table):
    if cast_table:
        (hn_ref, wq_ref, keys_ref, tab_ref, idx_ref, g_ref, tab_out_ref,
         v_scr, i_scr, tv_scr, fp_scr, idxt_scr, gt_scr) = refs
        tab_out_ref[...] = tab_ref[...].astype(BF16)
    else:
        (hn_ref, wq_ref, keys_ref, idx_ref, g_ref,
         v_scr, i_scr, tv_scr, fp_scr, idxt_scr, gt_scr) = refs
    tb = hn_ref.shape[0]
    k = PEER_TOPK
    q = _dot(hn_ref[...], wq_ref[...]).astype(BF16)
    sub = lax.broadcasted_iota(I32, (SUBLANES, LANES), 0).astype(F32)
    n_tiles = N_KEYS // SUBLANES
    neg = -jnp.inf

    for h in range(PEER_HEADS):
        scores = [_dot_nt(keys_ref[2 * h + p], q[:, (2 * h + p) * PEER_QHALF:(2 * h + p + 1) * PEER_QHALF])
                  for p in range(2)]
        for g in range(tb // LANES):
            ln = slice(g * LANES, (g + 1) * LANES)
            for p in range(2):
                s = scores[p][:, ln]
                vals = [s[SUBLANES * v:SUBLANES * (v + 1), :] for v in range(n_tiles)]
                ids = [sub + float(SUBLANES * v) for v in range(n_tiles)]
                for i, j in _sort_network(n_tiles):
                    a, b, ia, ib = vals[i], vals[j], ids[i], ids[j]
                    swap = (b > a) | ((b == a) & (ib < ia))
                    vals[i], vals[j] = jnp.where(swap, b, a), jnp.where(swap, a, b)
                    ids[i], ids[j] = jnp.where(swap, ib, ia), jnp.where(swap, ia, ib)
                for r in range(k):
                    m = _col_max(vals[0])
                    first = _col_min(jnp.where(vals[0] == m, ids[0], float(N_KEYS)))
                    sel = ids[0] == first
                    v_scr[p, r:r + 1, ln] = m
                    i_scr[p, r:r + 1, ln] = first
                    for d in range(k - 1 - r):
                        vals[d] = jnp.where(sel, vals[d + 1], vals[d])
                        ids[d] = jnp.where(sel, ids[d + 1], ids[d])

            lists = []
            for r in range(k):
                val = v_scr[0, r:r + 1, ln] + v_scr[1, 0:SUBLANES, ln]
                n_valid = k // (r + 1)
                lists.append(val if n_valid >= SUBLANES else jnp.where(sub < float(n_valid), val, neg))
            extra = v_scr[0, 0:1, ln] + v_scr[1, SUBLANES:k, ln]
            flat = sub
            flat_extra = sub + float(SUBLANES)
            for j in range(k):
                m = _col_max(jnp.maximum(lists[0], extra))
                first = _col_min(jnp.minimum(jnp.where(lists[0] == m, flat, float(k * k)),
                                             jnp.where(extra == m, flat_extra, float(k * k))))
                tv_scr[j:j + 1, ln] = m
                fp_scr[j:j + 1, ln] = first
                left = k - 1 - j
                if left:
                    sel = flat == first
                    for d in range(left):
                        lists[d] = jnp.where(sel, lists[d + 1], lists[d])
                    flat = jnp.where(sel, flat + float(k), flat)
                    extra = jnp.where(flat_extra == first, neg, extra)

        fp = fp_scr[...]
        pr = jnp.floor(fp * (1.0 / k))
        pc = fp - pr * float(k)
        e1 = jnp.zeros_like(fp)
        e2 = jnp.zeros_like(fp)
        for r in range(k):
            e1 = jnp.where(pr == float(r), i_scr[0, r:r + 1, :], e1)
            e2 = jnp.where(pc == float(r), i_scr[1, r:r + 1, :], e2)
        idxt_scr[h * k:(h + 1) * k, :] = (e1 * float(N_KEYS) + e2).astype(I32)
        tv = tv_scr[...]
        ex = jnp.exp(tv - tv[0:1, :])
        gt_scr[h * k:(h + 1) * k, :] = ex / jnp.sum(ex, axis=0, keepdims=True)

    idx_ref[...] = idxt_scr[...].T
    g_ref[...] = gt_scr[...].T


def _topk(hn2, wq_bf, keys_bf, table=None):
    t = hn2.shape[0]
    tb = TB_TOPK
    hk = PEER_HEADS * PEER_TOPK
    steps = t // tb
    in_specs = [pl.BlockSpec((tb, D_MODEL), lambda i: (i, 0)),
                pl.BlockSpec((D_MODEL, PEER_Q), lambda i: (0, 0)),
                pl.BlockSpec((2 * PEER_HEADS, N_KEYS, PEER_QHALF), lambda i: (0, 0, 0))]
    out_specs = [pl.BlockSpec((tb, hk), lambda i: (i, 0)), pl.BlockSpec((tb, hk), lambda i: (i, 0))]
    out_shape = [_sds((t, hk), I32), _sds((t, hk), F32)]
    args = [hn2, wq_bf, keys_bf]
    if table is not None:
        rows = table.shape[0] // steps
        assert rows * steps == table.shape[0] and rows % 16 == 0
        in_specs.append(pl.BlockSpec((rows, table.shape[1]), lambda i: (i, 0)))
        out_specs.append(pl.BlockSpec((rows, table.shape[1]), lambda i: (i, 0)))
        out_shape.append(_sds(table.shape, BF16))
        args.append(table)
    return pl.pallas_call(
        functools.partial(_topk_kernel, cast_table=table is not None),
        grid=(steps,),
        in_specs=in_specs,
        out_specs=out_specs,
        out_shape=out_shape,
        scratch_shapes=[pltpu.VMEM((2, PEER_TOPK, tb), F32),
                        pltpu.VMEM((2, PEER_TOPK, tb), F32),
                        pltpu.VMEM((PEER_TOPK, tb), F32),
                        pltpu.VMEM((PEER_TOPK, tb), F32),
                        pltpu.VMEM((hk, tb), I32),
                        pltpu.VMEM((hk, tb), F32)],
        compiler_params=_params("arbitrary"),
        name="peer_topk",
    )(*args)


def _stream_copy(src_hbm, buf, sem, step, block):
    slot = lax.rem(step, STREAM_SLOTS)
    return pltpu.make_async_copy(src_hbm.at[pl.ds(block * E_BLOCK, E_BLOCK)], buf.at[slot], sem.at[slot])


def _stream_prefetch(copy_for_step, step, total):
    @pl.when(step == 0)
    def _():
        for k in range(STREAM_AHEAD):
            copy_for_step(k).start()

    @pl.when(step + STREAM_AHEAD < total)
    def _():
        copy_for_step(step + STREAM_AHEAD).start()


def _peer_up_kernel(hn_ref, u_hbm, idx_ref, p_ref, s_even, s_odd, ubuf, usem):
    s = pl.program_id(0)
    total = pl.num_programs(0)
    n_blocks = N_EXPERTS // E_BLOCK
    per = E_BLOCK // N_KEYS

    def u_copy(step):
        return _stream_copy(u_hbm, ubuf, usem, step, lax.rem(jnp.minimum(step, total - 2), n_blocks))

    _stream_prefetch(u_copy, s, total)

    @pl.when(s == 0)
    def _():
        p_ref[...] = jnp.zeros_like(p_ref)
        s_odd[...] = jnp.zeros_like(s_odd)

    u_copy(s).wait()
    slot = lax.rem(s, STREAM_SLOTS)
    prev_step = jnp.maximum(s - 1, 0)
    row0 = pl.multiple_of(lax.div(prev_step, n_blocks) * TB_UP, TB_UP)
    first_row = jnp.where(s == 0, -per, lax.rem(prev_step, n_blocks) * per)

    def step(cur, prev):
        cur[...] = _dot_nt(hn_ref[...], ubuf[slot].astype(BF16))
        for c in range(TB_UP // SUBLANES):
            rows = pl.ds(row0 + c * SUBLANES, SUBLANES)
            idx = idx_ref[rows, :]
            i1 = idx >> KEY_BITS
            i2 = idx & (N_KEYS - 1)
            acc = p_ref[rows, :]
            for k in range(per):
                picked = jnp.take_along_axis(
                    prev[c * SUBLANES:(c + 1) * SUBLANES, k * N_KEYS:(k + 1) * N_KEYS], i2, axis=1)
                acc = jnp.where(i1 == first_row + k, picked, acc)
            p_ref[rows, :] = acc

    @pl.when(s % 2 == 0)
    def _():
        step(s_even, s_odd)

    @pl.when(s % 2 == 1)
    def _():
        step(s_odd, s_even)


def _peer_up(hn2, u, idx):
    t = hn2.shape[0]
    hk = PEER_HEADS * PEER_TOPK
    n_blocks = N_EXPERTS // E_BLOCK
    nb = t // TB_UP
    whole = pl.BlockSpec((t, hk), lambda s: (0, 0))
    return pl.pallas_call(
        _peer_up_kernel,
        grid=(nb * n_blocks + 1,),
        in_specs=[pl.BlockSpec((TB_UP, D_MODEL), lambda s: (jnp.minimum(s // n_blocks, nb - 1), 0)),
                  pl.BlockSpec(memory_space=pl.ANY),
                  whole],
        out_specs=whole,
        out_shape=_sds((t, hk), F32),
        scratch_shapes=[pltpu.VMEM((TB_UP, E_BLOCK), F32), pltpu.VMEM((TB_UP, E_BLOCK), F32),
                        pltpu.VMEM((STREAM_SLOTS, E_BLOCK, D_MODEL), u.dtype),
                        pltpu.SemaphoreType.DMA((STREAM_SLOTS,))],
        compiler_params=_params("arbitrary"),
        name="peer_up",
    )(hn2, u, idx)


def _peer_down_kernel(p_ref, g_ref, idx_ref, v_hbm, x1_ref, mod_ref, fg_ref, y_ref, a_scr, abuf, vbuf, vsem):
    j = pl.program_id(1)
    nj = pl.num_programs(1)
    step_id = pl.program_id(0) * nj + j

    def v_copy(s):
        return _stream_copy(v_hbm, vbuf, vsem, s, lax.rem(s, nj))

    _stream_prefetch(v_copy, step_id, pl.num_programs(0) * nj)
    v_copy(step_id).wait()

    @pl.when(j == 0)
    def _():
        y_ref[...] = jnp.zeros_like(y_ref)
        a_scr[...] = jax.nn.gelu(p_ref[...]) * g_ref[...]
        key_ids = lax.broadcasted_iota(I32, (N_KEYS, PEER_HEADS * PEER_TOPK), 0)

        def body(tt, carry):
            for u in range(SCATTER_UNROLL):
                t = tt * SCATTER_UNROLL + u
                idx = idx_ref[pl.ds(t, 1), :]
                pm = jnp.where((idx >> KEY_BITS) == key_ids, a_scr[pl.ds(t, 1), :], 0.0).astype(BF16)
                qm = jnp.where((idx & (N_KEYS - 1)) == key_ids, 1.0, 0.0).astype(BF16)
                abuf[pl.ds(pl.multiple_of(t * A_STRIDE, SUBLANES), N_KEYS), :] = _dot_nt(pm, qm)
            return carry

        lax.fori_loop(0, TB_DOWN // SCATTER_UNROLL, body, 0)

    per = E_BLOCK // N_KEYS
    cols = [abuf[pl.ds(j * per + k, TB_DOWN, stride=A_STRIDE), :] for k in range(per)]
    a_blk = jnp.concatenate(cols, axis=1).astype(BF16)
    y_ref[...] += _dot(a_blk, vbuf[lax.rem(step_id, STREAM_SLOTS)])

    @pl.when(j == nj - 1)
    def _():
        x2 = x1_ref[...] + mod_ref[5:6, :] * y_ref[...]
        y_ref[...] = _rms(x2) * fg_ref[...]


def _peer_down(p, g, idx, v_bf, x1, mod, mod_row, final_g):
    t = p.shape[0]
    hk = PEER_HEADS * PEER_TOPK
    tok = pl.BlockSpec((TB_DOWN, hk), lambda i, j: (i, 0))
    wide = pl.BlockSpec((TB_DOWN, D_MODEL), lambda i, j: (i, 0))
    return pl.pallas_call(
        _peer_down_kernel,
        grid=(t // TB_DOWN, N_EXPERTS // E_BLOCK),
        in_specs=[tok, tok, tok, pl.BlockSpec(memory_space=pl.ANY), wide,
                  pl.BlockSpec((None, N_MOD, D_MODEL), lambda i, j: (mod_row(i * TB_DOWN), 0, 0)),
                  pl.BlockSpec((1, D_MODEL), lambda i, j: (0, 0))],
        out_specs=wide,
        out_shape=_sds((t, D_MODEL), F32),
        scratch_shapes=[pltpu.VMEM((TB_DOWN, hk), F32),
                        pltpu.VMEM((TB_DOWN * A_STRIDE, N_KEYS), F32),
                        pltpu.VMEM((STREAM_SLOTS, E_BLOCK, D_MODEL), v_bf.dtype),
                        pltpu.SemaphoreType.DMA((STREAM_SLOTS,))],
        compiler_params=_params("arbitrary", "arbitrary"),
        name="peer_down",
    )(p, g, idx, v_bf, x1, mod, final_g)


def _grid_pos_embed(n_tokens):
    rows = n_tokens // GRID_W
    r = np.repeat(np.arange(rows, dtype=np.float64), GRID_W)
    col = np.tile(np.arange(GRID_W, dtype=np.float64), rows)
    quarter = D_MODEL // 4
    omega = 1.0 / (POS_THETA ** (np.arange(quarter, dtype=np.float64) / quarter))
    ar = r[:, None] * omega
    ac = col[:, None] * omega
    table = np.concatenate([np.sin(ar), np.cos(ar), np.sin(ac), np.cos(ac)], axis=-1)
    return jnp.asarray(table.astype(np.float32))


def _trunk(x, pos, mod, mod_base, per_seq, s0, lb, w, v_bf):
    n_seq, seq_len, _ = x.shape
    x2d = x.reshape(n_seq * seq_len, D_MODEL)

    def mod_row(tok):
        return mod_base + (tok // seq_len if per_seq else 0)

    z = _inproj(x2d, pos, mod, mod_row, w["norm1_g"], w["w_in"], seq_len)
    o_fw, o_bw, st_fin = _hgrn(z, lb, s0, n_seq, seq_len)
    x1, hn2 = _merge(x2d, pos, z, o_fw, o_bw, mod, mod_row, seq_len, w["sgu_norm_g"], w["w_spatial"],
                     w["b_spatial_t"], w["hgrn_norm_g"], w["w_proj_a"], w["w_proj_b"], w["w_out"],
                     w["norm2_g"])
    if v_bf is None:
        idx, gate, v_bf = _topk(hn2, w["peer_w_q"], w["peer_sub_keys"], table=w["peer_v"])
    else:
        idx, gate = _topk(hn2, w["peer_w_q"], w["peer_sub_keys"])
    p = _peer_up(hn2, w["peer_u"], idx)
    y = _peer_down(p, gate, idx, v_bf, x1, mod, mod_row, w["final_norm_g"])
    return y.reshape(x.shape), st_fin, v_bf


def kernel(x_prompt, x_sample, state_hgrn, c, c_ctx, w_ada, b_ada, norm1_g, w_in, sgu_norm_g,
           w_spatial, b_spatial, hgrn_lb, hgrn_norm_g, w_proj_a, w_proj_b, w_out, norm2_g,
           peer_w_q, peer_sub_keys, peer_u, peer_v, final_norm_g):
    assert w_in.shape[0] == 1, "single-layer stack"
    n_p, l_p, _ = x_prompt.shape
    n_s, l_s, _ = x_sample.shape
    assert n_s + 1 <= SUBLANES
    lb = jnp.cumsum(jax.nn.softmax(hgrn_lb.astype(F32), axis=0), axis=0)[0]

    cond = jnp.zeros((SUBLANES, D_MODEL), F32).at[0].set(c_ctx).at[1:1 + n_s].set(c)
    mod, w_in_bf = _adaln(cond, w_ada[0], b_ada[0][None], w_in[0])
    mod = mod.reshape(SUBLANES, N_MOD, D_MODEL)

    w = dict(
        norm1_g=norm1_g[0][None], w_in=w_in_bf, sgu_norm_g=sgu_norm_g[0][None],
        w_spatial=w_spatial[0].astype(BF16), b_spatial_t=b_spatial[0].T,
        hgrn_norm_g=hgrn_norm_g[0].reshape(1, H_B * DV),
        w_proj_a=w_proj_a[0].astype(BF16), w_proj_b=w_proj_b[0].astype(BF16),
        w_out=w_out[0].astype(BF16), norm2_g=norm2_g[0][None],
        peer_w_q=peer_w_q[0].astype(BF16),
        peer_sub_keys=peer_sub_keys[0].reshape(2 * PEER_HEADS, N_KEYS, PEER_QHALF).astype(BF16),
        peer_u=peer_u[0], peer_v=peer_v[0],
        final_norm_g=final_norm_g[None],
    )

    s0_lat = state_hgrn[:, 0].astype(F32)
    pos = _grid_pos_embed(l_s)

    y_prompt, s_ctx, v_bf = _trunk(x_prompt, None, mod, 0, False, None, lb, w, None)
    y_sample, _, _ = _trunk(x_sample, pos, mod, 1, True, s0_lat, lb, w, v_bf)
    return (y_prompt, y_sample, s_ctx[:, None])
```
